```python
import math
import jax, jax.numpy as jnp
from jax import lax
import numpy as np

D_MODEL = 2048
BATCH = 8
SEQ = 4096
DEPTH = 2
DEC_BATCH = 16
DEC_SEQ = 32
PAST_LEN = 4096

CHUNK = 64
N_MIXERS = 2
N_A_LAYERS = (DEPTH + 1) // 2
N_B_LAYERS = DEPTH // 2
SSM_GROUP_WIDTH = 16
SSM_GROUPS = D_MODEL // SSM_GROUP_WIDTH
SSM_STATE = 64
SCAN_BLOCK = 128
DT_MIN = 1e-3
DT_MAX = 1e-1
N_HEADS = 16
Q_LORA = 512
KV_LORA = 512
QK_NOPE = 128
QK_ROPE = 64
V_DIM = 128
ROPE_THETA = 10000.0
Q_BLOCK = 128
ATTN_SCALE = (QK_NOPE + QK_ROPE) ** -0.5
N_EXPERTS = 64
N_EXPERT_GROUPS = 8
EXPERTS_PER_GROUP = N_EXPERTS // N_EXPERT_GROUPS
TOPK_GROUPS = 1
TOP_K = 2
D_FF_EXPERT = 512
MOE_BLOCK = 128
DN_ALPHA = (2 * DEPTH) ** 0.25
DN_BETA = (8 * DEPTH) ** -0.25
LN_EPS = 1e-5
RMS_EPS = 1e-6
NEG_INF = -1e30

kernel_name = 'streaming_s5_mla_moe_deepnorm_adaln_step'


def _layer_norm(x, g, b):
    xf = x.astype(jnp.float32)
    mu = xf.mean(-1, keepdims=True)
    xc = xf - mu
    var = (xc * xc).mean(-1, keepdims=True)
    return (xc * lax.rsqrt(var + LN_EPS) * g.astype(jnp.float32) + b.astype(jnp.float32)).astype(x.dtype)


def _rms_norm(x, g):
    xf = x.astype(jnp.float32)
    ms = (xf * xf).mean(-1, keepdims=True)
    return (xf * lax.rsqrt(ms + RMS_EPS) * g.astype(jnp.float32)).astype(x.dtype)


def _rope_tables(pos, dtype):
    inv = 1.0 / (ROPE_THETA ** (jnp.arange(0, QK_ROPE, 2, dtype=jnp.float32) / QK_ROPE))
    ang = pos.astype(jnp.float32)[:, None] * inv[None, :]
    return jnp.cos(ang).astype(dtype), jnp.sin(ang).astype(dtype)


def _rope(x, cos, sin):
    half = x.shape[-1] // 2
    x1, x2 = x[..., :half], x[..., half:]
    return jnp.concatenate([x1 * cos - x2 * sin, x2 * cos + x1 * sin], axis=-1)


def _s5_mixer(u, h0_re, h0_im, a_re, a_im, log_dt, b_re, b_im, c_re, c_im, d_skip, w_glu):
    f32 = jnp.float32
    bsz, L, _ = u.shape
    dt = jnp.exp(log_dt.astype(f32))[:, None]
    lam_re, lam_im = a_re.astype(f32), a_im.astype(f32)
    mag = jnp.exp(lam_re * dt)
    abar_re, abar_im = mag * jnp.cos(lam_im * dt), mag * jnp.sin(lam_im * dt)
    den = lam_re * lam_re + lam_im * lam_im
    n_re, n_im = abar_re - 1.0, abar_im
    f_re = (n_re * lam_re + n_im * lam_im) / den
    f_im = (n_im * lam_re - n_re * lam_im) / den
    br, bi = b_re.astype(f32), b_im.astype(f32)
    bb_re = f_re[..., None] * br - f_im[..., None] * bi
    bb_im = f_re[..., None] * bi + f_im[..., None] * br
    cr, ci = c_re.astype(f32), c_im.astype(f32)
    blk = L if L <= SCAN_BLOCK else SCAN_BLOCK
    nblk = L // blk
    u_blocks = u.astype(f32).reshape(bsz, nblk, blk, SSM_GROUPS, SSM_GROUP_WIDTH).swapaxes(0, 1)
    a_blk_re = jnp.broadcast_to(abar_re, (bsz, blk, SSM_GROUPS, SSM_STATE))
    a_blk_im = jnp.broadcast_to(abar_im, (bsz, blk, SSM_GROUPS, SSM_STATE))

    def combine(e1, e2):
        a1r, a1i, b1r, b1i = e1
        a2r, a2i, b2r, b2i = e2
        return (a2r * a1r - a2i * a1i, a2r * a1i + a2i * a1r,
                a2r * b1r - a2i * b1i + b2r, a2r * b1i + a2i * b1r + b2i)

    def step(carry, ub):
        h_re0, h_im0 = carry
        x_re = jnp.einsum('btgw,gpw->btgp', ub, bb_re)
        x_im = jnp.einsum('btgw,gpw->btgp', ub, bb_im)
        p_re, p_im, h_re, h_im = lax.associative_scan(combine, (a_blk_re, a_blk_im, x_re, x_im), axis=1)
        h_re, h_im = (h_re + p_re * h_re0[:, None] - p_im * h_im0[:, None],
                      h_im + p_re * h_im0[:, None] + p_im * h_re0[:, None])
        y = jnp.einsum('btgp,gwp->btgw', h_re, cr) - jnp.einsum('btgp,gwp->btgw', h_im, ci)
        return (h_re[:, -1], h_im[:, -1]), y

    (hf_re, hf_im), ys = lax.scan(step, (h0_re.astype(f32), h0_im.astype(f32)), u_blocks)
    y = ys.swapaxes(0, 1).reshape(bsz, L, D_MODEL) + d_skip.astype(f32) * u.astype(f32)
    z = jax.nn.gelu(y).astype(u.dtype)
    val, gate = jnp.split(z @ w_glu, 2, axis=-1)
    return val * jax.nn.sigmoid(gate), hf_re.astype(h0_re.dtype), hf_im.astype(h0_im.dtype)


def _mla_project(x, pos, w_in, q_norm, kv_norm, w_uq):
    h = x @ w_in
    cq = _rms_norm(h[..., :Q_LORA], q_norm)
    ckv = _rms_norm(h[..., Q_LORA:Q_LORA + KV_LORA], kv_norm)
    kr = h[..., Q_LORA + KV_LORA:]
    q = jnp.einsum('blc,chd->blhd', cq, w_uq)
    cos, sin = _rope_tables(pos, x.dtype)
    q_nope = q[..., :QK_NOPE]
    q_rope = _rope(q[..., QK_NOPE:], cos[:, None, :], sin[:, None, :])
    kr = _rope(kr, cos, sin)
    return q_nope, q_rope, ckv, kr


def _mla_prompt_attention(q_nope, q_rope, ckv, kr, w_uk, w_uv):
    bsz, L = q_nope.shape[0], q_nope.shape[1]
    k_nope = jnp.einsum('bkc,chn->bkhn', ckv, w_uk)
    v = jnp.einsum('bkc,chd->bkhd', ckv, w_uv)
    k_chunk = jnp.arange(L) // CHUNK
    nq = L // Q_BLOCK
    qn = q_nope.reshape(bsz, nq, Q_BLOCK, N_HEADS, QK_NOPE).swapaxes(0, 1)
    qr = q_rope.reshape(bsz, nq, Q_BLOCK, N_HEADS, QK_ROPE).swapaxes(0, 1)

    def block(args):
        qn_b, qr_b, i = args
        s = (jnp.einsum('bqhn,bkhn->bhqk', qn_b, k_nope)
             + jnp.einsum('bqhr,bkr->bhqk', qr_b, kr)).astype(jnp.float32) * ATTN_SCALE
        q_chunk = (i * Q_BLOCK + jnp.arange(Q_BLOCK)) // CHUNK
        mask = k_chunk[None, :] <= q_chunk[:, None]
        p = jax.nn.softmax(jnp.where(mask, s, NEG_INF), axis=-1).astype(v.dtype)
        return jnp.einsum('bhqk,bkhd->bqhd', p, v)

    o = lax.map(block, (qn, qr, jnp.arange(nq)))
    return o.swapaxes(0, 1).reshape(bsz, L, N_HEADS * V_DIM)


def _mla_cached_attention(q_nope, q_rope, q_pos, ckv_all, kr_all, w_uk, w_uv):
    bsz, L = q_nope.shape[0], q_nope.shape[1]
    q_lat = jnp.einsum('bqhn,chn->bqhc', q_nope, w_uk)
    s = (jnp.einsum('bqhc,bkc->bhqk', q_lat, ckv_all)
         + jnp.einsum('bqhr,bkr->bhqk', q_rope, kr_all)).astype(jnp.float32) * ATTN_SCALE
    k_pos = jnp.arange(ckv_all.shape[1])
    mask = (k_pos // CHUNK)[None, :] <= (q_pos // CHUNK)[:, None]
    p = jax.nn.softmax(jnp.where(mask, s, NEG_INF), axis=-1).astype(ckv_all.dtype)
    o_lat = jnp.einsum('bhqk,bkc->bqhc', p, ckv_all)
    o = jnp.einsum('bqhc,chd->bqhd', o_lat, w_uv)
    return o.reshape(bsz, L, N_HEADS * V_DIM)


def _moe(x, w_router, router_bias, w_gate, w_up, w_down):
    f32 = jnp.float32
    T, dm = x.shape
    scores = jax.nn.sigmoid((x @ w_router).astype(f32))
    sel = scores + router_bias.astype(f32)
    grp = sel.reshape(T, N_EXPERT_GROUPS, EXPERTS_PER_GROUP)
    g_score = lax.top_k(grp, 2)[0].sum(-1)
    _, g_idx = lax.top_k(g_score, TOPK_GROUPS)
    g_mask = jax.nn.one_hot(g_idx, N_EXPERT_GROUPS, dtype=f32).sum(1) > 0
    e_mask = jnp.repeat(g_mask, EXPERTS_PER_GROUP, axis=1)
    _, e_idx = lax.top_k(jnp.where(e_mask, sel, NEG_INF), TOP_K)
    w = jnp.take_along_axis(scores, e_idx, axis=1)
    w = w / w.sum(-1, keepdims=True)
    n_assign = T * TOP_K
    flat_e = e_idx.reshape(-1)
    flat_tok = jnp.repeat(jnp.arange(T, dtype=jnp.int32), TOP_K)
    flat_w = w.reshape(-1)
    order = jnp.argsort(flat_e)
    se, stok, sw = flat_e[order], flat_tok[order], flat_w[order]
    counts = jnp.bincount(flat_e, length=N_EXPERTS)
    starts = jnp.cumsum(counts) - counts
    pcounts = (counts + MOE_BLOCK - 1) // MOE_BLOCK * MOE_BLOCK
    pends = jnp.cumsum(pcounts)
    pstarts = pends - pcounts
    dest = pstarts[se] + jnp.arange(n_assign) - starts[se]
    n_blocks = -(-n_assign // MOE_BLOCK) + N_EXPERTS
    n_rows = n_blocks * MOE_BLOCK
    row_tok = jnp.zeros((n_rows,), jnp.int32).at[dest].set(stok)
    row_w = jnp.zeros((n_rows,), f32).at[dest].set(sw)
    block_e = jnp.minimum(jnp.searchsorted(pends, jnp.arange(n_blocks) * MOE_BLOCK, side='right'), N_EXPERTS - 1)
    xr = x[row_tok].reshape(n_blocks, MOE_BLOCK, dm)

    def expert_block(args):
        xb, e = args
        h = jax.nn.silu(xb @ w_gate[e]) * (xb @ w_up[e])
        return h @ w_down[e]

    yr = lax.map(expert_block, (xr, block_e)).reshape(n_rows, dm)
    y = jnp.zeros((T, dm), f32).at[row_tok].add(yr.astype(f32) * row_w[:, None])
    return y.astype(x.dtype)


def _trunk(x, c, pos, ssm_state, mla_cache, p):
    bsz = x.shape[0]
    ssm_re, ssm_im, ckv_rows, kr_rows = [], [], [], []
    cs = jax.nn.silu(c)
    for i in range(DEPTH):
        mod = cs @ p['w_ada'][i] + p['b_ada'][i]
        sh1, sc1, g1, sh2, sc2, g2 = jnp.split(mod[:, None, :], 6, axis=-1)
        h = x * (1 + sc1) + sh1
        j = i // N_MIXERS
        if i % N_MIXERS == 0:
            if ssm_state is None:
                h0_re = jnp.zeros((bsz, SSM_GROUPS, SSM_STATE), x.dtype)
                h0_im = jnp.zeros((bsz, SSM_GROUPS, SSM_STATE), x.dtype)
            else:
                h0_re, h0_im = ssm_state[0][j], ssm_state[1][j]
            m, hr, hi = _s5_mixer(h, h0_re, h0_im, p['ssm_a_re'][j], p['ssm_a_im'][j], p['ssm_log_dt'][j],
                                  p['ssm_b_re'][j], p['ssm_b_im'][j], p['ssm_c_re'][j], p['ssm_c_im'][j],
                                  p['ssm_d'][j], p['ssm_w_glu'][j])
            ssm_re.append(hr)
            ssm_im.append(hi)
        else:
            q_nope, q_rope, ckv, kr = _mla_project(h, pos, p['mla_w_in'][j], p['mla_q_norm'][j],
                                                   p['mla_kv_norm'][j], p['mla_w_uq'][j])
            if mla_cache is None:
                o = _mla_prompt_attention(q_nope, q_rope, ckv, kr, p['mla_w_uk'][j], p['mla_w_uv'][j])
            else:
                ckv_all = jnp.concatenate([mla_cache[0][j], ckv], axis=1)
                kr_all = jnp.concatenate([mla_cache[1][j], kr], axis=1)
                o = _mla_cached_attention(q_nope, q_rope, pos, ckv_all, kr_all, p['mla_w_uk'][j], p['mla_w_uv'][j])
            m = o @ p['mla_w_o'][j]
            ckv_rows.append(ckv)
            kr_rows.append(kr)
        x = _layer_norm(DN_ALPHA * x + g1 * m, p['ln_g'][i, 0], p['ln_b'][i, 0])
        h = x * (1 + sc2) + sh2
        f = _moe(h.reshape(-1, D_MODEL), p['w_router'], p['router_bias'], p['moe_w_gate'][i],
                 p['moe_w_up'][i], p['moe_w_down'][i]).reshape(x.shape)
        x = _layer_norm(DN_ALPHA * x + g2 * f, p['ln_g'][i, 1], p['ln_b'][i, 1])
    return x, jnp.stack(ssm_re), jnp.stack(ssm_im), jnp.stack(ckv_rows), jnp.stack(kr_rows)


def setup_inputs(seed: int = 0) -> dict:
    key = jax.random.key(seed)
    ks = iter(jax.random.split(key, 40))

    def nrm(shape, scale):
        return jax.random.normal(next(ks), shape, jnp.float32) * scale

    D = D_MODEL
    a_im0 = jnp.pi * jnp.arange(SSM_STATE, dtype=jnp.float32)
    return {
        'x_prompt': nrm((BATCH, SEQ, D), 1.0),
        'x_sample': nrm((DEC_BATCH, DEC_SEQ, D), 1.0),
        'c_prompt': nrm((BATCH, D), 1.0),
        'c_sample': nrm((DEC_BATCH, D), 1.0),
        'state_ssm_re': nrm((N_A_LAYERS, DEC_BATCH, SSM_GROUPS, SSM_STATE), 0.3),
        'state_ssm_im': nrm((N_A_LAYERS, DEC_BATCH, SSM_GROUPS, SSM_STATE), 0.3),
        'cache_ckv': nrm((N_B_LAYERS, DEC_BATCH, PAST_LEN, KV_LORA), 1.0),
        'cache_krope': nrm((N_B_LAYERS, DEC_BATCH, PAST_LEN, QK_ROPE), 1.0),
        'w_ada': nrm((DEPTH, D, 6 * D), 0.5 * D ** -0.5),
        'b_ada': nrm((DEPTH, 6 * D), 0.02),
        'ln_g': 1.0 + nrm((DEPTH, 2, D), 0.05),
        'ln_b': nrm((DEPTH, 2, D), 0.02),
        'ssm_a_re': -0.5 + nrm((N_A_LAYERS, SSM_GROUPS, SSM_STATE), 0.01),
        'ssm_a_im': a_im0 + nrm((N_A_LAYERS, SSM_GROUPS, SSM_STATE), 0.01),
        'ssm_log_dt': jax.random.uniform(next(ks), (N_A_LAYERS, SSM_GROUPS), jnp.float32,
                                         math.log(DT_MIN), math.log(DT_MAX)),
        'ssm_b_re': nrm((N_A_LAYERS, SSM_GROUPS, SSM_STATE, SSM_GROUP_WIDTH), (2 * SSM_GROUP_WIDTH) ** -0.5),
        'ssm_b_im': nrm((N_A_LAYERS, SSM_GROUPS, SSM_STATE, SSM_GROUP_WIDTH), (2 * SSM_GROUP_WIDTH) ** -0.5),
        'ssm_c_re': nrm((N_A_LAYERS, SSM_GROUPS, SSM_GROUP_WIDTH, SSM_STATE), 0.5),
        'ssm_c_im': nrm((N_A_LAYERS, SSM_GROUPS, SSM_GROUP_WIDTH, SSM_STATE), 0.5),
        'ssm_d': nrm((N_A_LAYERS, D), 1.0),
        'ssm_w_glu': jnp.concatenate([nrm((N_A_LAYERS, D, D), DN_BETA * D ** -0.5),
                                      nrm((N_A_LAYERS, D, D), D ** -0.5)], axis=-1),
        'mla_w_in': nrm((N_B_LAYERS, D, Q_LORA + KV_LORA + QK_ROPE), D ** -0.5),
        'mla_q_norm': 1.0 + nrm((N_B_LAYERS, Q_LORA), 0.05),
        'mla_kv_norm': 1.0 + nrm((N_B_LAYERS, KV_LORA), 0.05),
        'mla_w_uq': nrm((N_B_LAYERS, Q_LORA, N_HEADS, QK_NOPE + QK_ROPE), Q_LORA ** -0.5),
        'mla_w_uk': nrm((N_B_LAYERS, KV_LORA, N_HEADS, QK_NOPE), KV_LORA ** -0.5),
        'mla_w_uv': nrm((N_B_LAYERS, KV_LORA, N_HEADS, V_DIM), KV_LORA ** -0.5),
        'mla_w_o': nrm((N_B_LAYERS, N_HEADS * V_DIM, D), DN_BETA * (N_HEADS * V_DIM) ** -0.5),
        'w_router': nrm((D, N_EXPERTS), D ** -0.5),
        'router_bias': nrm((N_EXPERTS,), 0.01),
        'moe_w_gate': nrm((DEPTH, N_EXPERTS, D, D_FF_EXPERT), D ** -0.5),
        'moe_w_up': nrm((DEPTH, N_EXPERTS, D, D_FF_EXPERT), D ** -0.5),
        'moe_w_down': nrm((DEPTH, N_EXPERTS, D_FF_EXPERT, D), DN_BETA * D_FF_EXPERT ** -0.5),
    }


def reference(x_prompt, x_sample, c_prompt, c_sample, state_ssm_re, state_ssm_im, cache_ckv, cache_krope,
              w_ada, b_ada, ln_g, ln_b, ssm_a_re, ssm_a_im, ssm_log_dt, ssm_b_re, ssm_b_im, ssm_c_re, ssm_c_im,
              ssm_d, ssm_w_glu, mla_w_in, mla_q_norm, mla_kv_norm, mla_w_uq, mla_w_uk, mla_w_uv, mla_w_o,
              w_router, router_bias, moe_w_gate, moe_w_up, moe_w_down):
    params = dict(w_ada=w_ada, b_ada=b_ada, ln_g=ln_g, ln_b=ln_b,
                  ssm_a_re=ssm_a_re, ssm_a_im=ssm_a_im, ssm_log_dt=ssm_log_dt,
                  ssm_b_re=ssm_b_re, ssm_b_im=ssm_b_im, ssm_c_re=ssm_c_re, ssm_c_im=ssm_c_im,
                  ssm_d=ssm_d, ssm_w_glu=ssm_w_glu,
                  mla_w_in=mla_w_in, mla_q_norm=mla_q_norm, mla_kv_norm=mla_kv_norm, mla_w_uq=mla_w_uq,
                  mla_w_uk=mla_w_uk, mla_w_uv=mla_w_uv, mla_w_o=mla_w_o,
                  w_router=w_router, router_bias=router_bias,
                  moe_w_gate=moe_w_gate, moe_w_up=moe_w_up, moe_w_down=moe_w_down)
    pos_prompt = jnp.arange(x_prompt.shape[1])
    pos_sample = cache_ckv.shape[2] + jnp.arange(x_sample.shape[1])
    y_prompt, ssm_re_p, ssm_im_p, ckv_p, kr_p = _trunk(x_prompt, c_prompt, pos_prompt, None, None, params)
    y_sample, ssm_re_s, ssm_im_s, ckv_s, kr_s = _trunk(x_sample, c_sample, pos_sample,
                                                       (state_ssm_re, state_ssm_im),
                                                       (cache_ckv, cache_krope), params)
    return (y_prompt, y_sample, ssm_re_p, ssm_im_p, ckv_p, kr_p, ssm_re_s, ssm_im_s, ckv_s, kr_s)
```

```python
import functools

import jax
import jax.numpy as jnp
from jax import lax
from jax.experimental import pallas as pl
from jax.experimental.pallas import tpu as pltpu

F32 = jnp.float32
BF16 = jnp.bfloat16

CHUNK = 64
SSM_GROUP_WIDTH = 16
N_EXPERT_GROUPS = 8
TOP_K = 2
ROPE_THETA = 10000.0
LN_EPS = 1e-5
RMS_EPS = 1e-6
NEG_INF = -1e30

V7X_VMEM_BYTES = 64 * 1024 * 1024
SUBLANES = 8
LANES = 128
MXU_DIM = 256

VMEM_LIMIT = 56 * 1024 * 1024
S5_CHANNELS = MXU_DIM
S5_TIME = 128
ROW_TILE = 256
ATTN_BLOCK = 256
MOE_ROWS = 256
CACHE_KEYS = 1024


def _cparams(sem):
    return pltpu.CompilerParams(dimension_semantics=sem, vmem_limit_bytes=VMEM_LIMIT)


def _bdot(a, b):
    return jnp.dot(a, b, preferred_element_type=F32)


def _ada_kernel(c_ref, w_ref, b_ref, o_ref):
    c = c_ref[...]
    cs = (c * jax.nn.sigmoid(c)).astype(BF16)
    o_ref[...] = _bdot(cs, w_ref[...].astype(BF16)) + b_ref[...]


def _ada_mod(c_all, w_ada, b_ada):
    depth, d, n = w_ada.shape
    bc = c_all.shape[0]
    tn = 1024
    return pl.pallas_call(
        _ada_kernel,
        grid=(depth, n // tn),
        in_specs=[pl.BlockSpec((bc, d), lambda i, j: (0, 0)),
                  pl.BlockSpec((None, d, tn), lambda i, j: (i, 0, j)),
                  pl.BlockSpec((None, 1, tn), lambda i, j: (i, 0, j))],
        out_specs=pl.BlockSpec((None, bc, tn), lambda i, j: (i, 0, j)),
        out_shape=jax.ShapeDtypeStruct((depth, bc, n), F32),
        compiler_params=_cparams(("parallel", "parallel")),
        name="ada_mod",
    )(c_all, w_ada, b_ada.reshape(depth, 1, n))


def _modulate_kernel(x_ref, sc_ref, sh_ref, o_ref):
    o_ref[...] = x_ref[...] * (1.0 + sc_ref[...]) + sh_ref[...]


def _modulate_to_s5_layout(x, sc, sh):
    b, l, d = x.shape
    tl = min(l, 512)
    out = pl.pallas_call(
        _modulate_kernel,
        grid=(b, l // tl),
        in_specs=[pl.BlockSpec((None, tl, d), lambda i, t: (i, t, 0)),
                  pl.BlockSpec((None, 1, d), lambda i, t: (i, 0, 0)),
                  pl.BlockSpec((None, 1, d), lambda i, t: (i, 0, 0))],
        out_specs=pl.BlockSpec((None, tl, d), lambda i, t: (i // SUBLANES, t, i % SUBLANES)),
        out_shape=jax.ShapeDtypeStruct((b // SUBLANES, l, SUBLANES * d), F32),
        compiler_params=_cparams(("parallel", "parallel")),
        name="modulate_s5",
    )(x, sc, sh)
    return out.reshape(b // SUBLANES, l, SUBLANES, d)


def _s5_kernel(u_ref, bd_ref, cd_ref, ab_ref, d_ref, h0r_ref, h0i_ref,
               z_ref, hr_ref, hi_ref, xs_ref, h_ref, *, tt, sb):
    t = pl.program_id(2)

    @pl.when(t == 0)
    def _():
        h_ref[0] = h0r_ref[...]
        h_ref[1] = h0i_ref[...]

    cb = u_ref.shape[-1]
    u = u_ref[...].reshape(tt * SUBLANES, cb)
    xs_ref[...] = _bdot(u.astype(BF16), bd_ref[...])
    a_re = jnp.broadcast_to(ab_ref[0:1, :], (SUBLANES, sb))
    a_im = jnp.broadcast_to(ab_ref[1:2, :], (SUBLANES, sb))

    def step(i, carry):
        h_re, h_im = carry
        r0 = pl.multiple_of(i * SUBLANES, SUBLANES)
        x_re = xs_ref[pl.ds(r0, SUBLANES), 0:sb]
        x_im = xs_ref[pl.ds(r0, SUBLANES), sb:2 * sb]
        n_re = a_re * h_re - a_im * h_im + x_re
        n_im = a_re * h_im + a_im * h_re + x_im
        xs_ref[pl.ds(r0, SUBLANES), 0:sb] = n_re
        xs_ref[pl.ds(r0, SUBLANES), sb:2 * sb] = n_im
        return n_re, n_im

    h_re, h_im = lax.fori_loop(0, tt, step, (h_ref[0], h_ref[1]))
    h_ref[0] = h_re
    h_ref[1] = h_im
    y = _bdot(xs_ref[...].astype(BF16), cd_ref[...]) + d_ref[...] * u
    z_ref[...] = jax.nn.gelu(y).astype(z_ref.dtype)

    @pl.when(t == pl.num_programs(2) - 1)
    def _():
        hr_ref[...] = h_re
        hi_ref[...] = h_im


def _s5_scan(u, bd, cd, ab, d_skip, h0_re, h0_im):
    nbg, l, _, d = u.shape
    nj, cb, sb2 = bd.shape
    sb = sb2 // 2
    tt = min(l, S5_TIME)
    nt = l // tt
    kern = functools.partial(_s5_kernel, tt=tt, sb=sb)
    state_spec = pl.BlockSpec((None, SUBLANES, sb), lambda g, j, t: (g, 0, j))
    state_shape = jax.ShapeDtypeStruct((nbg, SUBLANES, nj * sb), F32)
    return pl.pallas_call(
        kern,
        grid=(nbg, nj, nt),
        in_specs=[pl.BlockSpec((None, tt, SUBLANES, cb), lambda g, j, t: (g, t, 0, j)),
                  pl.BlockSpec((None, cb, sb2), lambda g, j, t: (j, 0, 0)),
                  pl.BlockSpec((None, sb2, cb), lambda g, j, t: (j, 0, 0)),
                  pl.BlockSpec((None, 2, sb), lambda g, j, t: (j, 0, 0)),
                  pl.BlockSpec((1, cb), lambda g, j, t: (0, j)),
                  state_spec, state_spec],
        out_specs=[pl.BlockSpec((tt * SUBLANES, cb), lambda g, j, t: (g * nt + t, j)),
                   state_spec, state_spec],
        out_shape=[jax.ShapeDtypeStruct((nbg * l * SUBLANES, d), BF16), state_shape, state_shape],
        scratch_shapes=[pltpu.VMEM((tt * SUBLANES, sb2), F32), pltpu.VMEM((2, SUBLANES, sb), F32)],
        compiler_params=_cparams(("parallel", "parallel", "arbitrary")),
        name="s5_scan",
    )(u, bd, cd, ab, d_skip, h0_re, h0_im)


def _s5_tables(a_re, a_im, log_dt, b_re, b_im, c_re, c_im):
    g, p = a_re.shape
    w = b_re.shape[-1]
    gl = S5_CHANNELS // w
    nj = g // gl
    dt = jnp.exp(log_dt)[:, None]
    mag = jnp.exp(a_re * dt)
    abar_re, abar_im = mag * jnp.cos(a_im * dt), mag * jnp.sin(a_im * dt)
    den = a_re * a_re + a_im * a_im
    n_re, n_im = abar_re - 1.0, abar_im
    f_re = (n_re * a_re + n_im * a_im) / den
    f_im = (n_im * a_re - n_re * a_im) / den
    bb_re = f_re[..., None] * b_re - f_im[..., None] * b_im
    bb_im = f_re[..., None] * b_im + f_im[..., None] * b_re
    eye = jnp.eye(gl, dtype=F32)
    bb = jnp.stack([bb_re, bb_im]).reshape(2, nj, gl, p, w)
    bd = jnp.einsum('sjgpw,gh->jgwshp', bb, eye).reshape(nj, gl * w, 2 * gl * p)
    cc = jnp.stack([c_re, -c_im]).reshape(2, nj, gl, w, p)
    cd = jnp.einsum('sjgwp,gh->jsgphw', cc, eye).reshape(nj, 2 * gl * p, gl * w)
    ab = jnp.stack([abar_re.reshape(nj, gl * p), abar_im.reshape(nj, gl * p)], axis=1)
    return bd.astype(BF16), cd.astype(BF16), ab


def _layer_norm(y, g, b):
    mu = jnp.mean(y, axis=-1, keepdims=True)
    yc = y - mu
    var = jnp.mean(yc * yc, axis=-1, keepdims=True)
    return yc * lax.rsqrt(var + LN_EPS) * g + b


def _mix_ln_kernel(a_ref, w_ref, x_ref, gate_ref, lng_ref, lnb_ref, sc_ref, sh_ref,
                   x1_ref, h_ref, *, glu, alpha):
    acc = _bdot(a_ref[...], w_ref[...])
    if glu:
        d = acc.shape[-1] // 2
        m = acc[:, :d] * jax.nn.sigmoid(acc[:, d:])
    else:
        m = acc
    x1 = _layer_norm(alpha * x_ref[...] + gate_ref[...] * m, lng_ref[...], lnb_ref[...])
    x1_ref[...] = x1
    h_ref[...] = (x1 * (1.0 + sc_ref[...]) + sh_ref[...]).astype(h_ref.dtype)


def _mix_ln(a, a_spec, w, x, gate, ln_g, ln_b, sc, sh, *, glu, alpha):
    b, l, d = x.shape
    tm = min(l, ROW_TILE)
    k, n = w.shape
    row = pl.BlockSpec((None, 1, d), lambda i, t: (i, 0, 0))
    vec = pl.BlockSpec((1, d), lambda i, t: (0, 0))
    tile = pl.BlockSpec((None, tm, d), lambda i, t: (i, t, 0))
    kern = functools.partial(_mix_ln_kernel, glu=glu, alpha=alpha)
    return pl.pallas_call(
        kern,
        grid=(b, l // tm),
        in_specs=[a_spec(tm), pl.BlockSpec((k, n), lambda i, t: (0, 0), pipeline_mode=pl.Buffered(1)),
                  tile, row, vec, vec, row, row],
        out_specs=[tile, tile],
        out_shape=[jax.ShapeDtypeStruct((b, l, d), F32), jax.ShapeDtypeStruct((b, l, d), BF16)],
        compiler_params=_cparams(("parallel", "parallel")),
        name="mix_ln_glu" if glu else "mix_ln_proj",
    )(a, w, x, gate, ln_g, ln_b, sc, sh)


def _ffn_ln_kernel(f_ref, x_ref, gate_ref, lng_ref, lnb_ref, *rest, alpha):
    x2 = _layer_norm(alpha * x_ref[...] + gate_ref[...] * f_ref[...], lng_ref[...], lnb_ref[...])
    if len(rest) == 1:
        rest[0][...] = x2
    else:
        sc_ref, sh_ref, x2_ref, h_ref = rest
        x2_ref[...] = x2
        h_ref[...] = (x2 * (1.0 + sc_ref[...]) + sh_ref[...]).astype(h_ref.dtype)


def _ffn_ln(f_all, row0, x, gate, ln_g, ln_b, next_mod, *, alpha):
    b, l, d = x.shape
    tm = min(l, ROW_TILE)
    nt = l // tm
    blk0 = row0 // tm
    row = pl.BlockSpec((None, 1, d), lambda i, t: (i, 0, 0))
    vec = pl.BlockSpec((1, d), lambda i, t: (0, 0))
    tile = pl.BlockSpec((None, tm, d), lambda i, t: (i, t, 0))
    in_specs = [pl.BlockSpec((tm, d), lambda i, t: (blk0 + i * nt + t, 0)), tile, row, vec, vec]
    x2_shape = jax.ShapeDtypeStruct((b, l, d), F32)
    if next_mod is None:
        args, out_specs, out_shape = (), tile, x2_shape
    else:
        args, in_specs = tuple(next_mod), in_specs + [row, row]
        out_specs, out_shape = [tile, tile], [x2_shape, jax.ShapeDtypeStruct((b, l, d), BF16)]
    out = pl.pallas_call(
        functools.partial(_ffn_ln_kernel, alpha=alpha),
        grid=(b, nt),
        in_specs=in_specs,
        out_specs=out_specs,
        out_shape=out_shape,
        compiler_params=_cparams(("parallel", "parallel")),
        name="ffn_ln",
    )(f_all, x, gate, ln_g, ln_b, *args)
    return (out, None) if next_mod is None else out


def _rms_norm(x, g):
    ms = jnp.mean(x * x, axis=-1, keepdims=True)
    return x * lax.rsqrt(ms + RMS_EPS) * g


def _mla_in_kernel(h_ref, w_ref, qn_ref, kvn_ref, cos_ref, sin_ref,
                   cq_ref, ckv_ref, kr_ref, krp_ref, *, ql, kvl, rope):
    acc = _bdot(h_ref[...], w_ref[...])
    cq_ref[...] = _rms_norm(acc[:, :ql], qn_ref[...]).astype(cq_ref.dtype)
    ckv_ref[...] = _rms_norm(acc[:, ql:ql + kvl], kvn_ref[...])
    o = ql + kvl
    kr = acc[:, o:o + LANES] * cos_ref[...] + acc[:, o + LANES:o + 2 * LANES] * sin_ref[...]
    kr_ref[...] = kr[:, :rope]
    krp_ref[...] = kr.astype(krp_ref.dtype)


def _mla_in(h, w_in_ext, q_norm, kv_norm, cos_t, sin_t, *, ql, kvl, rope):
    b, l, d = h.shape
    tm = min(l, ROW_TILE)
    n = w_in_ext.shape[1]
    kern = functools.partial(_mla_in_kernel, ql=ql, kvl=kvl, rope=rope)
    tab = pl.BlockSpec((tm, LANES), lambda i, t: (t, 0))

    def out(nf):
        return pl.BlockSpec((None, tm, nf), lambda i, t: (i, t, 0))

    return pl.pallas_call(
        kern,
        grid=(b, l // tm),
        in_specs=[pl.BlockSpec((None, tm, d), lambda i, t: (i, t, 0)),
                  pl.BlockSpec((d, n), lambda i, t: (0, 0)),
                  pl.BlockSpec((1, ql), lambda i, t: (0, 0)),
                  pl.BlockSpec((1, kvl), lambda i, t: (0, 0)), tab, tab],
        out_specs=[out(ql), out(kvl), out(rope), out(LANES)],
        out_shape=[jax.ShapeDtypeStruct((b, l, ql), BF16), jax.ShapeDtypeStruct((b, l, kvl), F32),
                   jax.ShapeDtypeStruct((b, l, rope), F32), jax.ShapeDtypeStruct((b, l, LANES), BF16)],
        compiler_params=_cparams(("parallel", "parallel")),
        name="mla_in",
    )(h, w_in_ext, q_norm, kv_norm, cos_t, sin_t)


def _mla_q_kernel(cq_ref, w_ref, cos_ref, sin_ref, q_ref, *, nope, scale):
    acc = _bdot(cq_ref[...], w_ref[...])
    q_ref[:, :nope] = (acc[:, :nope] * scale).astype(q_ref.dtype)
    qr = acc[:, nope:nope + LANES] * cos_ref[...] + acc[:, nope + LANES:nope + 2 * LANES] * sin_ref[...]
    q_ref[:, nope:] = (qr * scale).astype(q_ref.dtype)


def _mla_q(cq, wq, cos_t, sin_t, *, nope, scale):
    b, l, ql = cq.shape
    nh, _, n = wq.shape
    tm = min(l, ROW_TILE)
    tab = pl.BlockSpec((tm, LANES), lambda i, t, h: (t, 0))
    return pl.pallas_call(
        functools.partial(_mla_q_kernel, nope=nope, scale=scale),
        grid=(b, l // tm, nh),
        in_specs=[pl.BlockSpec((None, tm, ql), lambda i, t, h: (i, t, 0)),
                  pl.BlockSpec((None, ql, n), lambda i, t, h: (h, 0, 0)), tab, tab],
        out_specs=pl.BlockSpec((None, None, tm, nope + LANES), lambda i, t, h: (i, h, t, 0)),
        out_shape=jax.ShapeDtypeStruct((b, nh, l, nope + LANES), BF16),
        compiler_params=_cparams(("parallel", "parallel", "arbitrary")),
        name="mla_q",
    )(cq, wq, cos_t, sin_t)


def _mla_kv_kernel(ckv_ref, krp_ref, w_ref, k_ref, v_ref, *, nope):
    acc = _bdot(ckv_ref[...].astype(BF16), w_ref[...])
    k_ref[:, :nope] = acc[:, :nope].astype(k_ref.dtype)
    k_ref[:, nope:] = krp_ref[...]
    v_ref[...] = acc[:, nope:].astype(v_ref.dtype)


def _mla_kv(ckv, krp, wkv, *, nope):
    b, l, kvl = ckv.shape
    nh, _, n = wkv.shape
    vd = n - nope
    tm = min(l, ROW_TILE)
    return pl.pallas_call(
        functools.partial(_mla_kv_kernel, nope=nope),
        grid=(b, l // tm, nh),
        in_specs=[pl.BlockSpec((None, tm, kvl), lambda i, t, h: (i, t, 0)),
                  pl.BlockSpec((None, tm, LANES), lambda i, t, h: (i, t, 0)),
                  pl.BlockSpec((None, kvl, n), lambda i, t, h: (h, 0, 0))],
        out_specs=[pl.BlockSpec((None, None, tm, nope + LANES), lambda i, t, h: (i, h, t, 0)),
                   pl.BlockSpec((None, None, tm, vd), lambda i, t, h: (i, h, t, 0))],
        out_shape=[jax.ShapeDtypeStruct((b, nh, l, nope + LANES), BF16),
                   jax.ShapeDtypeStruct((b, nh, l, vd), BF16)],
        compiler_params=_cparams(("parallel", "parallel", "arbitrary")),
        name="mla_kv",
    )(ckv, krp, wkv)


def _qk(q, k):
    return lax.dot_general(q, k, (((1,), (1,)), ((), ())), preferred_element_type=F32)


def _softmax_step(carry, s, v):
    m, l, acc = carry
    m_new = jnp.maximum(m, jnp.max(s, axis=-1, keepdims=True))
    p = jnp.exp(s - m_new)
    alpha = jnp.exp(m - m_new)
    l = alpha * l + jnp.sum(p, axis=-1, keepdims=True)
    acc = alpha * acc + _bdot(p.astype(BF16), v)
    return m_new, l, acc


def _prompt_attn_kernel(q_ref, k_ref, v_ref, o_ref, *, blk):
    i = pl.program_id(2)
    q = q_ref[...]
    vd = v_ref.shape[-1]

    def body(j, carry):
        r0 = pl.multiple_of(j * blk, blk)
        return _softmax_step(carry, _qk(q, k_ref[pl.ds(r0, blk), :]), v_ref[pl.ds(r0, blk), :])

    init = (jnp.full((blk, 1), NEG_INF, F32), jnp.zeros((blk, 1), F32), jnp.zeros((blk, vd), F32))
    carry = lax.fori_loop(0, i, body, init)
    r0 = pl.multiple_of(i * blk, blk)
    s = _qk(q, k_ref[pl.ds(r0, blk), :])
    q_chunk = lax.broadcasted_iota(jnp.int32, (blk, blk), 0) // CHUNK
    k_chunk = lax.broadcasted_iota(jnp.int32, (blk, blk), 1) // CHUNK
    s = jnp.where(k_chunk <= q_chunk, s, NEG_INF)
    _, l, acc = _softmax_step(carry, s, v_ref[pl.ds(r0, blk), :])
    o_ref[...] = (acc / l).astype(o_ref.dtype)


def _prompt_attn(q, k, v):
    b, nh, l, dk = q.shape
    vd = v.shape[-1]
    blk = min(l, ATTN_BLOCK)
    return pl.pallas_call(
        functools.partial(_prompt_attn_kernel, blk=blk),
        grid=(b, nh, l // blk),
        in_specs=[pl.BlockSpec((None, None, blk, dk), lambda i, h, t: (i, h, t, 0)),
                  pl.BlockSpec((None, None, l, dk), lambda i, h, t: (i, h, 0, 0)),
                  pl.BlockSpec((None, None, l, vd), lambda i, h, t: (i, h, 0, 0))],
        out_specs=pl.BlockSpec((None, blk, vd), lambda i, h, t: (i, t, h)),
        out_shape=jax.ShapeDtypeStruct((b, l, nh * vd), BF16),
        compiler_params=_cparams(("parallel", "parallel", "arbitrary")),
        name="prompt_attn",
    )(q, k, v)


def _absorb_q_kernel(q_ref, w_ref, o_ref, *, nope):
    b, lq, dk = q_ref.shape
    q = q_ref[...].reshape(b * lq, dk)[:, :nope]
    o_ref[...] = _bdot(q, w_ref[...]).reshape(b, lq, -1).astype(o_ref.dtype)


def _absorb_q(q, w_ukt, *, nope):
    b, nh, lq, dk = q.shape
    kvl = w_ukt.shape[-1]
    return pl.pallas_call(
        functools.partial(_absorb_q_kernel, nope=nope),
        grid=(nh,),
        in_specs=[pl.BlockSpec((b, None, lq, dk), lambda h: (0, h, 0, 0)),
                  pl.BlockSpec((None, nope, kvl), lambda h: (h, 0, 0))],
        out_specs=pl.BlockSpec((b, None, lq, kvl), lambda h: (0, h, 0, 0)),
        out_shape=jax.ShapeDtypeStruct((b, nh, lq, kvl), BF16),
        compiler_params=_cparams(("parallel",)),
        name="absorb_q",
    )(q, w_ukt)


def _cached_attn_kernel(ql_ref, q_ref, cc_ref, ck_ref, nc_ref, nk_ref, o_ref, *, nope, rope, past, kb):
    nh, lq, kvl = ql_ref.shape
    rows = nh * lq
    ql = ql_ref[...].reshape(rows, kvl)
    qr = q_ref[...].reshape(rows, -1)[:, nope:nope + rope]
    q_pos = past + lax.broadcasted_iota(jnp.int32, (rows, 1), 0) % lq
    carry = (jnp.full((rows, 1), NEG_INF, F32), jnp.zeros((rows, 1), F32), jnp.zeros((rows, kvl), F32))

    def scores(c, kr, k0):
        s = _qk(ql, c) + _qk(qr, kr)
        k_pos = k0 + lax.broadcasted_iota(jnp.int32, s.shape, 1)
        return jnp.where(k_pos // CHUNK <= q_pos // CHUNK, s, NEG_INF)

    for j in range(past // kb):
        c = cc_ref[j * kb:(j + 1) * kb, :].astype(BF16)
        kr = ck_ref[j * kb:(j + 1) * kb, :].astype(BF16)
        carry = _softmax_step(carry, scores(c, kr, j * kb), c)
    c = nc_ref[...].astype(BF16)
    kr = nk_ref[...].astype(BF16)
    _, l, acc = _softmax_step(carry, scores(c, kr, past), c)
    o_ref[...] = (acc / l).reshape(nh, lq, kvl).astype(o_ref.dtype)


def _cached_attn(q_lat, q, cache_ckv, cache_kr, new_ckv, new_kr, *, nope, rope):
    b, nh, lq, kvl = q_lat.shape
    past = cache_ckv.shape[1]
    kb = min(past, CACHE_KEYS)
    dk = q.shape[-1]
    kern = functools.partial(_cached_attn_kernel, nope=nope, rope=rope, past=past, kb=kb)
    return pl.pallas_call(
        kern,
        grid=(b,),
        in_specs=[pl.BlockSpec((None, nh, lq, kvl), lambda i: (i, 0, 0, 0)),
                  pl.BlockSpec((None, nh, lq, dk), lambda i: (i, 0, 0, 0)),
                  pl.BlockSpec((None, past, kvl), lambda i: (i, 0, 0)),
                  pl.BlockSpec((None, past, rope), lambda i: (i, 0, 0)),
                  pl.BlockSpec((None, lq, kvl), lambda i: (i, 0, 0)),
                  pl.BlockSpec((None, lq, rope), lambda i: (i, 0, 0))],
        out_specs=pl.BlockSpec((None, nh, lq, kvl), lambda i: (i, 0, 0, 0)),
        out_shape=jax.ShapeDtypeStruct((b, nh, lq, kvl), BF16),
        compiler_params=_cparams(("parallel",)),
        name="cached_attn",
    )(q_lat, q, cache_ckv, cache_kr, new_ckv, new_kr)


def _absorb_v_kernel(o_ref, w_ref, out_ref):
    b, lq, kvl = o_ref.shape
    out_ref[...] = _bdot(o_ref[...].reshape(b * lq, kvl), w_ref[...]).reshape(b, lq, -1).astype(out_ref.dtype)


def _absorb_v(o_lat, w_uv):
    b, nh, lq, kvl = o_lat.shape
    vd = w_uv.shape[-1]
    return pl.pallas_call(
        _absorb_v_kernel,
        grid=(nh,),
        in_specs=[pl.BlockSpec((b, None, lq, kvl), lambda h: (0, h, 0, 0)),
                  pl.BlockSpec((None, kvl, vd), lambda h: (h, 0, 0))],
        out_specs=pl.BlockSpec((b, lq, vd), lambda h: (0, 0, h)),
        out_shape=jax.ShapeDtypeStruct((b, lq, nh * vd), BF16),
        compiler_params=_cparams(("parallel",)),
        name="absorb_v",
    )(o_lat, w_uv)


def _router_kernel(x_ref, w_ref, o_ref):
    o_ref[...] = _bdot(x_ref[...], w_ref[...])


def _router_logits(x, w):
    t, d = x.shape
    e = w.shape[1]
    tm = 1024
    return pl.pallas_call(
        _router_kernel,
        grid=(pl.cdiv(t, tm),),
        in_specs=[pl.BlockSpec((tm, d), lambda i: (i, 0)), pl.BlockSpec((d, e), lambda i: (0, 0))],
        out_specs=pl.BlockSpec((tm, e), lambda i: (i, 0)),
        out_shape=jax.ShapeDtypeStruct((t, e), F32),
        compiler_params=_cparams(("parallel",)),
        name="router_logits",
    )(x, w)


def _expert_kernel(be_ref, x_ref, wg_ref, wu_ref, wd_ref, o_ref):
    del be_ref
    x = x_ref[...]
    g = _bdot(x, wg_ref[...])
    h = (g * jax.nn.sigmoid(g)) * _bdot(x, wu_ref[...])
    o_ref[...] = _bdot(h.astype(BF16), wd_ref[...])


def _expert_mlp(block_e, xr, w_gate, w_up, w_down):
    n_rows, d = xr.shape
    ff = w_gate.shape[-1]
    bm = MOE_ROWS
    grid_spec = pltpu.PrefetchScalarGridSpec(
        num_scalar_prefetch=1,
        grid=(n_rows // bm,),
        in_specs=[pl.BlockSpec((bm, d), lambda i, be: (i, 0)),
                  pl.BlockSpec((None, d, ff), lambda i, be: (be[i], 0, 0)),
                  pl.BlockSpec((None, d, ff), lambda i, be: (be[i], 0, 0)),
                  pl.BlockSpec((None, ff, d), lambda i, be: (be[i], 0, 0))],
        out_specs=pl.BlockSpec((bm, d), lambda i, be: (i, 0)),
    )
    return pl.pallas_call(
        _expert_kernel,
        grid_spec=grid_spec,
        out_shape=jax.ShapeDtypeStruct((n_rows, d), F32),
        compiler_params=_cparams(("arbitrary",)),
        name="expert_mlp",
    )(block_e, xr, w_gate, w_up, w_down)


def _first_argmax(x):
    return jnp.argmax(x, axis=-1).astype(jnp.int32)


def _moe(h, w_router, router_bias, w_gate, w_up, w_down):
    t, d = h.shape
    ne = w_router.shape[1]
    ng = N_EXPERT_GROUPS
    eg = ne // ng
    scores = jax.nn.sigmoid(_router_logits(h, w_router))
    sel = scores + router_bias
    grp = sel.reshape(t, ng, eg)
    lane = jnp.arange(eg, dtype=jnp.int32)
    i1 = _first_argmax(grp)
    m1 = jnp.max(grp, axis=-1)
    m2 = jnp.max(jnp.where(lane == i1[..., None], -jnp.inf, grp), axis=-1)
    g_idx = _first_argmax(m1 + m2)
    sel_g = jnp.take_along_axis(grp, g_idx[:, None, None], axis=1)[:, 0]
    sc_g = jnp.take_along_axis(scores.reshape(t, ng, eg), g_idx[:, None, None], axis=1)[:, 0]
    e1 = _first_argmax(sel_g)
    e2 = _first_argmax(jnp.where(lane == e1[:, None], -jnp.inf, sel_g))
    w1 = jnp.take_along_axis(sc_g, e1[:, None], axis=1)[:, 0]
    w2 = jnp.take_along_axis(sc_g, e2[:, None], axis=1)[:, 0]
    wsum = w1 + w2
    e_idx = jnp.stack([g_idx * eg + e1, g_idx * eg + e2], axis=1)
    w = jnp.stack([w1 / wsum, w2 / wsum], axis=1)

    bm = MOE_ROWS
    n_assign = t * TOP_K
    flat_e = e_idx.reshape(-1)
    order = jnp.argsort(flat_e)
    se = flat_e[order]
    counts = jnp.bincount(flat_e, length=ne)
    starts = jnp.cumsum(counts) - counts
    pcounts = (counts + bm - 1) // bm * bm
    pends = jnp.cumsum(pcounts)
    pstarts = pends - pcounts
    dest_sorted = pstarts[se] + jnp.arange(n_assign) - starts[se]
    n_blocks = -(-n_assign // bm) + ne
    n_rows = n_blocks * bm
    row_tok = jnp.zeros((n_rows,), jnp.int32).at[dest_sorted].set((order // TOP_K).astype(jnp.int32))
    dest = jnp.zeros((n_assign,), jnp.int32).at[order].set(dest_sorted.astype(jnp.int32)).reshape(t, TOP_K)
    block_e = jnp.minimum(jnp.searchsorted(pends, jnp.arange(n_blocks) * bm, side='right'), ne - 1)
    yr = _expert_mlp(block_e.astype(jnp.int32), h[row_tok], w_gate, w_up, w_down)
    return w[:, 0:1] * yr[dest[:, 0]] + w[:, 1:2] * yr[dest[:, 1]]


def _rope_tables(pos, rope):
    inv = 1.0 / (ROPE_THETA ** (jnp.arange(0, rope, 2, dtype=F32) / rope))
    ang = pos.astype(F32)[:, None] * inv[None, :]
    cos, sin = jnp.cos(ang), jnp.sin(ang)
    pad = jnp.zeros((pos.shape[0], LANES - rope), F32)
    return jnp.concatenate([cos, cos, pad], axis=1), jnp.concatenate([-sin, sin, pad], axis=1)


def kernel(x_prompt, x_sample, c_prompt, c_sample, state_ssm_re, state_ssm_im, cache_ckv, cache_krope, w_ada, b_ada, ln_g, ln_b, ssm_a_re, ssm_a_im, ssm_log_dt, ssm_b_re, ssm_b_im, ssm_c_re, ssm_c_im, ssm_d, ssm_w_glu, mla_w_in, mla_q_norm, mla_kv_norm, mla_w_uq, mla_w_uk, mla_w_uv, mla_w_o, w_router, router_bias, moe_w_gate, moe_w_up, moe_w_down):
    depth, d = w_ada.shape[0], w_ada.shape[1]
    alpha = (2 * depth) ** 0.25
    nh, nope, vd = mla_w_uk.shape[2], mla_w_uk.shape[3], mla_w_uv.shape[3]
    ql, kvl = mla_q_norm.shape[1], mla_kv_norm.shape[1]
    rope = mla_w_uq.shape[3] - nope
    scale = (nope + rope) ** -0.5
    n_groups, n_state = ssm_a_re.shape[1], ssm_a_re.shape[2]
    past = cache_ckv.shape[2]

    xs = [x_prompt, x_sample]
    bsz = [x.shape[0] for x in xs]
    seq = [x.shape[1] for x in xs]
    row0 = [0, bsz[0] * seq[0]]
    pos = [jnp.arange(seq[0]), past + jnp.arange(seq[1])]
    tabs = [_rope_tables(p, rope) for p in pos]

    mod = _ada_mod(jnp.concatenate([c_prompt, c_sample], axis=0), w_ada, b_ada)

    def mod_rows(i, grp, piece):
        b0 = 0 if grp == 0 else bsz[0]
        return mod[i, b0:b0 + bsz[grp], piece * d:(piece + 1) * d].reshape(bsz[grp], 1, d)

    assert depth % 2 == 0, "layers alternate S5 / MLA mixers, starting with S5"
    w_router_b = w_router.astype(BF16)
    half = rope // 2
    ssm_out = [[[], []], [[], []]]
    ckv_out, kr_out = [[], []], [[], []]

    h_in = [None, None]
    for i in range(depth):
        j = i // 2
        x1s, h2s = [], []
        if i % 2 == 0:
            bd, cd, ab = _s5_tables(ssm_a_re[j], ssm_a_im[j], ssm_log_dt[j], ssm_b_re[j], ssm_b_im[j],
                                    ssm_c_re[j], ssm_c_im[j])
            w_glu = ssm_w_glu[j].astype(BF16)
        else:
            w_in = mla_w_in[j]
            r0 = ql + kvl
            zc = jnp.zeros((d, LANES - rope), F32)
            w_in_ext = jnp.concatenate([w_in, zc, w_in[:, r0 + half:], w_in[:, r0:r0 + half], zc], axis=1).astype(BF16)
            wq = mla_w_uq[j]
            zq = jnp.zeros((ql, nh, LANES - rope), F32)
            wq = jnp.concatenate([wq, zq, wq[..., nope + half:], wq[..., nope:nope + half], zq], axis=-1)
            wq = wq.transpose(1, 0, 2).astype(BF16)
            wkv = jnp.concatenate([mla_w_uk[j], mla_w_uv[j]], axis=-1).transpose(1, 0, 2).astype(BF16)
            w_ukt = mla_w_uk[j].transpose(1, 2, 0).astype(BF16)
            w_uvh = mla_w_uv[j].transpose(1, 0, 2).astype(BF16)
            w_o = mla_w_o[j].astype(BF16)
        for grp in range(2):
            x = xs[grp]
            b, l = bsz[grp], seq[grp]
            sh1, sc1, g1, sh2, sc2 = (mod_rows(i, grp, p) for p in range(5))
            lng, lnb = ln_g[i, 0].reshape(1, d), ln_b[i, 0].reshape(1, d)
            if i % 2 == 0:
                u = _modulate_to_s5_layout(x, sc1, sh1)
                nbg = b // SUBLANES
                if grp == 0:
                    h0r = jnp.zeros((nbg, SUBLANES, n_groups * n_state), F32)
                    h0i = h0r
                else:
                    h0r = state_ssm_re[j].reshape(nbg, SUBLANES, n_groups * n_state)
                    h0i = state_ssm_im[j].reshape(nbg, SUBLANES, n_groups * n_state)
                z, hr, hi = _s5_scan(u, bd, cd, ab, ssm_d[j].reshape(1, d), h0r, h0i)
                ssm_out[grp][0].append(hr.reshape(b, n_groups, n_state))
                ssm_out[grp][1].append(hi.reshape(b, n_groups, n_state))
                z = z.reshape(nbg, l, SUBLANES * d)

                def a_spec(tm):
                    return pl.BlockSpec((None, tm, d), lambda bi, t: (bi // SUBLANES, t, bi % SUBLANES))

                x1, h2 = _mix_ln(z, a_spec, w_glu, x, g1, lng, lnb, sc2, sh2, glu=True, alpha=alpha)
            else:
                cos_t, sin_t = tabs[grp]
                cq, ckv, kr, krp = _mla_in(h_in[grp], w_in_ext, mla_q_norm[j].reshape(1, ql),
                                           mla_kv_norm[j].reshape(1, kvl), cos_t, sin_t, ql=ql, kvl=kvl, rope=rope)
                q = _mla_q(cq, wq, cos_t, sin_t, nope=nope, scale=scale)
                if grp == 0:
                    k, v = _mla_kv(ckv, krp, wkv, nope=nope)
                    o = _prompt_attn(q, k, v)
                else:
                    q_lat = _absorb_q(q, w_ukt, nope=nope)
                    o_lat = _cached_attn(q_lat, q, cache_ckv[j], cache_krope[j], ckv, kr, nope=nope, rope=rope)
                    o = _absorb_v(o_lat, w_uvh)
                ckv_out[grp].append(ckv)
                kr_out[grp].append(kr)

                def a_spec(tm):
                    return pl.BlockSpec((None, tm, nh * vd), lambda bi, t: (bi, t, 0))

                x1, h2 = _mix_ln(o, a_spec, w_o, x, g1, lng, lnb, sc2, sh2, glu=False, alpha=alpha)
            x1s.append(x1)
            h2s.append(h2)
        h2_all = jnp.concatenate([h.reshape(-1, d) for h in h2s], axis=0)
        f_all = _moe(h2_all, w_router_b, router_bias, moe_w_gate[i].astype(BF16), moe_w_up[i].astype(BF16),
                     moe_w_down[i].astype(BF16))
        for grp in range(2):
            g2 = mod_rows(i, grp, 5)
            next_mod = (mod_rows(i + 1, grp, 1), mod_rows(i + 1, grp, 0)) if i % 2 == 0 else None
            xs[grp], h_in[grp] = _ffn_ln(f_all, row0[grp], x1s[grp], g2, ln_g[i, 1].reshape(1, d),
                                         ln_b[i, 1].reshape(1, d), next_mod, alpha=alpha)

    return (xs[0], xs[1],
            jnp.stack(ssm_out[0][0]), jnp.stack(ssm_out[0][1]), jnp.stack(ckv_out[0]), jnp.stack(kr_out[0]),
            jnp.stack(ssm_out[1][0]), jnp.stack(ssm_out[1][1]), jnp.stack(ckv_out[1]), jnp.stack(kr_out[1]))
```

```python
import functools
import math

import jax
import jax.numpy as jnp
from jax import lax
from jax.experimental import pallas as pl
from jax.experimental.pallas import tpu as pltpu

F32 = jnp.float32
BF16 = jnp.bfloat16

CHUNK = 64
SSM_GROUP_WIDTH = 16
N_EXPERT_GROUPS = 8
TOP_K = 2
ROPE_THETA = 10000.0
LN_EPS = 1e-5
RMS_EPS = 1e-6
NEG_INF = -1e30

V7X_VMEM_BYTES = 64 * 1024 * 1024
SUBLANES = 8
LANES = 128
MXU_DIM = 256

VMEM_LIMIT = 56 * 1024 * 1024
S5_CHANNELS = MXU_DIM
S5_TIME = 128
ROW_TILE = 256
ATTN_BLOCK = 512
ATTN_HEADS = 2
MOE_ROWS = 256
CACHE_KEYS = 1024


def _cparams(sem):
    return pltpu.CompilerParams(dimension_semantics=sem, vmem_limit_bytes=VMEM_LIMIT)


def _bdot(a, b):
    return jnp.dot(a, b, preferred_element_type=F32)


def _ada_kernel(c_ref, w_ref, b_ref, o_ref):
    c = c_ref[...]
    cs = (c * jax.nn.sigmoid(c)).astype(BF16)
    o_ref[...] = _bdot(cs, w_ref[...].astype(BF16)) + b_ref[...]


def _ada_mod(c_all, w_ada, b_ada):
    depth, d, n = w_ada.shape
    bc = c_all.shape[0]
    tn = 1024
    return pl.pallas_call(
        _ada_kernel,
        grid=(depth, n // tn),
        in_specs=[pl.BlockSpec((bc, d), lambda i, j: (0, 0)),
                  pl.BlockSpec((None, d, tn), lambda i, j: (i, 0, j)),
                  pl.BlockSpec((None, 1, tn), lambda i, j: (i, 0, j))],
        out_specs=pl.BlockSpec((None, bc, tn), lambda i, j: (i, 0, j)),
        out_shape=jax.ShapeDtypeStruct((depth, bc, n), F32),
        compiler_params=_cparams(("parallel", "parallel")),
        name="ada_mod",
    )(c_all, w_ada, b_ada.reshape(depth, 1, n))


def _modulate_kernel(x_ref, sc_ref, sh_ref, o_ref):
    o_ref[...] = x_ref[...] * (1.0 + sc_ref[...]) + sh_ref[...]


def _modulate_to_s5_layout(x, sc, sh):
    b, l, d = x.shape
    tl = min(l, 512)
    out = pl.pallas_call(
        _modulate_kernel,
        grid=(b, l // tl),
        in_specs=[pl.BlockSpec((None, tl, d), lambda i, t: (i, t, 0)),
                  pl.BlockSpec((None, 1, d), lambda i, t: (i, 0, 0)),
                  pl.BlockSpec((None, 1, d), lambda i, t: (i, 0, 0))],
        out_specs=pl.BlockSpec((None, tl, d), lambda i, t: (i // SUBLANES, t, i % SUBLANES)),
        out_shape=jax.ShapeDtypeStruct((b // SUBLANES, l, SUBLANES * d), F32),
        compiler_params=_cparams(("parallel", "parallel")),
        name="modulate_s5",
    )(x, sc, sh)
    return out.reshape(b // SUBLANES, l, SUBLANES, d)


def _s5_kernel(u_ref, bd_ref, cd_ref, ab_ref, d_ref, h0r_ref, h0i_ref,
               z_ref, hr_ref, hi_ref, xs_ref, h_ref, *, tt, sb):
    t = pl.program_id(2)

    @pl.when(t == 0)
    def _():
        h_ref[0] = h0r_ref[...]
        h_ref[1] = h0i_ref[...]

    cb = u_ref.shape[-1]
    u = u_ref[...].reshape(tt * SUBLANES, cb)
    xs_ref[...] = _bdot(u.astype(BF16), bd_ref[...])
    a_re = jnp.broadcast_to(ab_ref[0:1, :], (SUBLANES, sb))
    a_im = jnp.broadcast_to(ab_ref[1:2, :], (SUBLANES, sb))

    def step(i, carry):
        h_re, h_im = carry
        r0 = pl.multiple_of(i * SUBLANES, SUBLANES)
        x_re = xs_ref[pl.ds(r0, SUBLANES), 0:sb]
        x_im = xs_ref[pl.ds(r0, SUBLANES), sb:2 * sb]
        n_re = a_re * h_re - a_im * h_im + x_re
        n_im = a_re * h_im + a_im * h_re + x_im
        xs_ref[pl.ds(r0, SUBLANES), 0:sb] = n_re
        xs_ref[pl.ds(r0, SUBLANES), sb:2 * sb] = n_im
        return n_re, n_im

    h_re, h_im = lax.fori_loop(0, tt, step, (h_ref[0], h_ref[1]))
    h_ref[0] = h_re
    h_ref[1] = h_im
    y = _bdot(xs_ref[...].astype(BF16), cd_ref[...]) + d_ref[...] * u
    z_ref[...] = jax.nn.gelu(y).astype(z_ref.dtype)

    @pl.when(t == pl.num_programs(2) - 1)
    def _():
        hr_ref[...] = h_re
        hi_ref[...] = h_im


def _s5_scan(u, bd, cd, ab, d_skip, h0_re, h0_im):
    nbg, l, _, d = u.shape
    nj, cb, sb2 = bd.shape
    sb = sb2 // 2
    tt = min(l, S5_TIME)
    nt = l // tt
    kern = functools.partial(_s5_kernel, tt=tt, sb=sb)
    state_spec = pl.BlockSpec((None, SUBLANES, sb), lambda g, j, t: (g, 0, j))
    state_shape = jax.ShapeDtypeStruct((nbg, SUBLANES, nj * sb), F32)
    return pl.pallas_call(
        kern,
        grid=(nbg, nj, nt),
        in_specs=[pl.BlockSpec((None, tt, SUBLANES, cb), lambda g, j, t: (g, t, 0, j)),
                  pl.BlockSpec((None, cb, sb2), lambda g, j, t: (j, 0, 0)),
                  pl.BlockSpec((None, sb2, cb), lambda g, j, t: (j, 0, 0)),
                  pl.BlockSpec((None, 2, sb), lambda g, j, t: (j, 0, 0)),
                  pl.BlockSpec((1, cb), lambda g, j, t: (0, j)),
                  state_spec, state_spec],
        out_specs=[pl.BlockSpec((tt * SUBLANES, cb), lambda g, j, t: (g * nt + t, j)),
                   state_spec, state_spec],
        out_shape=[jax.ShapeDtypeStruct((nbg * l * SUBLANES, d), BF16), state_shape, state_shape],
        scratch_shapes=[pltpu.VMEM((tt * SUBLANES, sb2), F32), pltpu.VMEM((2, SUBLANES, sb), F32)],
        compiler_params=_cparams(("parallel", "parallel", "arbitrary")),
        name="s5_scan",
    )(u, bd, cd, ab, d_skip, h0_re, h0_im)


def _s5_tables(a_re, a_im, log_dt, b_re, b_im, c_re, c_im):
    g, p = a_re.shape
    w = b_re.shape[-1]
    gl = S5_CHANNELS // w
    nj = g // gl
    dt = jnp.exp(log_dt)[:, None]
    mag = jnp.exp(a_re * dt)
    abar_re, abar_im = mag * jnp.cos(a_im * dt), mag * jnp.sin(a_im * dt)
    den = a_re * a_re + a_im * a_im
    n_re, n_im = abar_re - 1.0, abar_im
    f_re = (n_re * a_re + n_im * a_im) / den
    f_im = (n_im * a_re - n_re * a_im) / den
    bb_re = f_re[..., None] * b_re - f_im[..., None] * b_im
    bb_im = f_re[..., None] * b_im + f_im[..., None] * b_re
    eye = jnp.eye(gl, dtype=F32)
    bb = jnp.stack([bb_re, bb_im]).reshape(2, nj, gl, p, w)
    bd = jnp.einsum('sjgpw,gh->jgwshp', bb, eye).reshape(nj, gl * w, 2 * gl * p)
    cc = jnp.stack([c_re, -c_im]).reshape(2, nj, gl, w, p)
    cd = jnp.einsum('sjgwp,gh->jsgphw', cc, eye).reshape(nj, 2 * gl * p, gl * w)
    ab = jnp.stack([abar_re.reshape(nj, gl * p), abar_im.reshape(nj, gl * p)], axis=1)
    return bd.astype(BF16), cd.astype(BF16), ab


def _layer_norm(y, g, b):
    mu = jnp.mean(y, axis=-1, keepdims=True)
    yc = y - mu
    var = jnp.mean(yc * yc, axis=-1, keepdims=True)
    return yc * lax.rsqrt(var + LN_EPS) * g + b


def _mix_ln_kernel(a_ref, w_ref, x_ref, gate_ref, lng_ref, lnb_ref, sc_ref, sh_ref,
                   x1_ref, h_ref, *, glu, alpha):
    acc = _bdot(a_ref[...], w_ref[...])
    if glu:
        d = acc.shape[-1] // 2
        m = acc[:, :d] * jax.nn.sigmoid(acc[:, d:])
    else:
        m = acc
    x1 = _layer_norm(alpha * x_ref[...] + gate_ref[...] * m, lng_ref[...], lnb_ref[...])
    x1_ref[...] = x1
    h_ref[...] = (x1 * (1.0 + sc_ref[...]) + sh_ref[...]).astype(h_ref.dtype)


def _mix_ln(a, a_spec, w, x, gate, ln_g, ln_b, sc, sh, *, glu, alpha):
    b, l, d = x.shape
    tm = min(l, ROW_TILE)
    k, n = w.shape
    row = pl.BlockSpec((None, 1, d), lambda i, t: (i, 0, 0))
    vec = pl.BlockSpec((1, d), lambda i, t: (0, 0))
    tile = pl.BlockSpec((None, tm, d), lambda i, t: (i, t, 0))
    kern = functools.partial(_mix_ln_kernel, glu=glu, alpha=alpha)
    return pl.pallas_call(
        kern,
        grid=(b, l // tm),
        in_specs=[a_spec(tm), pl.BlockSpec((k, n), lambda i, t: (0, 0), pipeline_mode=pl.Buffered(1)),
                  tile, row, vec, vec, row, row],
        out_specs=[tile, tile],
        out_shape=[jax.ShapeDtypeStruct((b, l, d), F32), jax.ShapeDtypeStruct((b, l, d), BF16)],
        compiler_params=_cparams(("parallel", "parallel")),
        name="mix_ln_glu" if glu else "mix_ln_proj",
    )(a, w, x, gate, ln_g, ln_b, sc, sh)


def _ffn_ln_kernel(f_ref, x_ref, gate_ref, lng_ref, lnb_ref, *rest, alpha):
    x2 = _layer_norm(alpha * x_ref[...] + gate_ref[...] * f_ref[...], lng_ref[...], lnb_ref[...])
    if len(rest) == 1:
        rest[0][...] = x2
    else:
        sc_ref, sh_ref, x2_ref, h_ref = rest
        x2_ref[...] = x2
        h_ref[...] = (x2 * (1.0 + sc_ref[...]) + sh_ref[...]).astype(h_ref.dtype)


def _ffn_ln(f_all, row0, x, gate, ln_g, ln_b, next_mod, *, alpha):
    b, l, d = x.shape
    tm = min(l, ROW_TILE)
    nt = l // tm
    blk0 = row0 // tm
    row = pl.BlockSpec((None, 1, d), lambda i, t: (i, 0, 0))
    vec = pl.BlockSpec((1, d), lambda i, t: (0, 0))
    tile = pl.BlockSpec((None, tm, d), lambda i, t: (i, t, 0))
    in_specs = [pl.BlockSpec((tm, d), lambda i, t: (blk0 + i * nt + t, 0)), tile, row, vec, vec]
    x2_shape = jax.ShapeDtypeStruct((b, l, d), F32)
    if next_mod is None:
        args, out_specs, out_shape = (), tile, x2_shape
    else:
        args, in_specs = tuple(next_mod), in_specs + [row, row]
        out_specs, out_shape = [tile, tile], [x2_shape, jax.ShapeDtypeStruct((b, l, d), BF16)]
    out = pl.pallas_call(
        functools.partial(_ffn_ln_kernel, alpha=alpha),
        grid=(b, nt),
        in_specs=in_specs,
        out_specs=out_specs,
        out_shape=out_shape,
        compiler_params=_cparams(("parallel", "parallel")),
        name="ffn_ln",
    )(f_all, x, gate, ln_g, ln_b, *args)
    return (out, None) if next_mod is None else out


def _rms_norm(x, g):
    ms = jnp.mean(x * x, axis=-1, keepdims=True)
    return x * lax.rsqrt(ms + RMS_EPS) * g


def _mla_in_kernel(h_ref, w_ref, qn_ref, kvn_ref, cos_ref, sin_ref,
                   cq_ref, ckv_ref, kr_ref, krp_ref, *, ql, kvl, rope):
    acc = _bdot(h_ref[...], w_ref[...])
    cq_ref[...] = _rms_norm(acc[:, :ql], qn_ref[...]).astype(cq_ref.dtype)
    ckv_ref[...] = _rms_norm(acc[:, ql:ql + kvl], kvn_ref[...])
    o = ql + kvl
    kr = acc[:, o:o + LANES] * cos_ref[...] + acc[:, o + LANES:o + 2 * LANES] * sin_ref[...]
    kr_ref[...] = kr[:, :rope]
    krp_ref[...] = kr.astype(krp_ref.dtype)


def _mla_in(h, w_in_ext, q_norm, kv_norm, cos_t, sin_t, *, ql, kvl, rope):
    b, l, d = h.shape
    tm = min(l, ROW_TILE)
    n = w_in_ext.shape[1]
    kern = functools.partial(_mla_in_kernel, ql=ql, kvl=kvl, rope=rope)
    tab = pl.BlockSpec((tm, LANES), lambda i, t: (t, 0))

    def out(nf):
        return pl.BlockSpec((None, tm, nf), lambda i, t: (i, t, 0))

    return pl.pallas_call(
        kern,
        grid=(b, l // tm),
        in_specs=[pl.BlockSpec((None, tm, d), lambda i, t: (i, t, 0)),
                  pl.BlockSpec((d, n), lambda i, t: (0, 0)),
                  pl.BlockSpec((1, ql), lambda i, t: (0, 0)),
                  pl.BlockSpec((1, kvl), lambda i, t: (0, 0)), tab, tab],
        out_specs=[out(ql), out(kvl), out(rope), out(LANES)],
        out_shape=[jax.ShapeDtypeStruct((b, l, ql), BF16), jax.ShapeDtypeStruct((b, l, kvl), F32),
                   jax.ShapeDtypeStruct((b, l, rope), F32), jax.ShapeDtypeStruct((b, l, LANES), BF16)],
        compiler_params=_cparams(("parallel", "parallel")),
        name="mla_in",
    )(h, w_in_ext, q_norm, kv_norm, cos_t, sin_t)


def _mla_q_kernel(cq_ref, w_ref, cos_ref, sin_ref, q_ref, *, nh, scale):
    acc = _bdot(cq_ref[...], w_ref[...])
    cos = cos_ref[...] * scale
    sin = sin_ref[...] * scale
    for h in range(nh):
        q_ref[h, :, :LANES] = (acc[:, h * LANES:(h + 1) * LANES] * scale).astype(q_ref.dtype)
        r0, p0 = (nh + h) * LANES, (2 * nh + h) * LANES
        q_ref[h, :, LANES:] = (acc[:, r0:r0 + LANES] * cos + acc[:, p0:p0 + LANES] * sin).astype(q_ref.dtype)


def _mla_q(cq, wq, cos_t, sin_t, *, nh, scale):
    b, l, ql = cq.shape
    n = wq.shape[1]
    tm = min(l, ROW_TILE)
    tab = pl.BlockSpec((tm, LANES), lambda i, t: (t, 0))
    return pl.pallas_call(
        functools.partial(_mla_q_kernel, nh=nh, scale=scale),
        grid=(b, l // tm),
        in_specs=[pl.BlockSpec((None, tm, ql), lambda i, t: (i, t, 0)),
                  pl.BlockSpec((ql, n), lambda i, t: (0, 0)), tab, tab],
        out_specs=pl.BlockSpec((None, nh, tm, 2 * LANES), lambda i, t: (i, 0, t, 0)),
        out_shape=jax.ShapeDtypeStruct((b, nh, l, 2 * LANES), BF16),
        compiler_params=_cparams(("parallel", "parallel")),
        name="mla_q",
    )(cq, wq, cos_t, sin_t)


def _mla_kv_kernel(ckv_ref, krp_ref, w_ref, k_ref, v_ref, *, nh):
    acc = _bdot(ckv_ref[...].astype(BF16), w_ref[...])
    krp = krp_ref[...]
    lane = lax.broadcasted_iota(jnp.int32, krp.shape, 1)
    ones_col = jnp.where(lane == 0, 1.0, 0.0).astype(v_ref.dtype)
    for h in range(nh):
        k_ref[h, :, :LANES] = acc[:, h * LANES:(h + 1) * LANES].astype(k_ref.dtype)
        k_ref[h, :, LANES:] = krp
        v_ref[h, :, :LANES] = acc[:, (nh + h) * LANES:(nh + h + 1) * LANES].astype(v_ref.dtype)
        v_ref[h, :, LANES:] = ones_col


def _mla_kv(ckv, krp, wkv, *, nh):
    b, l, kvl = ckv.shape
    n = wkv.shape[1]
    tm = min(l, ROW_TILE)
    head_tile = pl.BlockSpec((None, nh, tm, 2 * LANES), lambda i, t: (i, 0, t, 0))
    head_shape = jax.ShapeDtypeStruct((b, nh, l, 2 * LANES), BF16)
    return pl.pallas_call(
        functools.partial(_mla_kv_kernel, nh=nh),
        grid=(b, l // tm),
        in_specs=[pl.BlockSpec((None, tm, kvl), lambda i, t: (i, t, 0)),
                  pl.BlockSpec((None, tm, LANES), lambda i, t: (i, t, 0)),
                  pl.BlockSpec((kvl, n), lambda i, t: (0, 0))],
        out_specs=[head_tile, head_tile],
        out_shape=[head_shape, head_shape],
        compiler_params=_cparams(("parallel", "parallel")),
        name="mla_kv",
    )(ckv, krp, wkv)


def _qk(q, k):
    return lax.dot_general(q, k, (((1,), (1,)), ((), ())), preferred_element_type=F32)


def _softmax_step(carry, s, v):
    m, l, acc = carry
    m_new = jnp.maximum(m, jnp.max(s, axis=-1, keepdims=True))
    p = jnp.exp2(s - m_new)
    alpha = jnp.exp2(m - m_new)
    l = alpha * l + jnp.sum(p, axis=-1, keepdims=True)
    acc = alpha * acc + _bdot(p.astype(BF16), v)
    return m_new, l, acc


def _softmax_step_ones(carry, s, v):
    m, acc = carry
    m_new = jnp.maximum(m, jnp.max(s, axis=-1, keepdims=True))
    p = jnp.exp2(s - m_new)
    acc = jnp.exp2(m - m_new) * acc + _bdot(p.astype(BF16), v)
    return m_new, acc


def _prompt_attn_kernel(q_ref, k_ref, v_ref, o_ref, *, blk, vd):
    i = pl.program_id(2)
    heads = range(q_ref.shape[0])
    qs = [q_ref[h] for h in heads]

    def body(j, carry):
        r0 = pl.multiple_of(j * blk, blk)
        return tuple(_softmax_step_ones(carry[h], _qk(qs[h], k_ref[h, pl.ds(r0, blk), :]),
                                        v_ref[h, pl.ds(r0, blk), :]) for h in heads)

    init = (jnp.full((blk, 1), NEG_INF, F32), jnp.zeros((blk, v_ref.shape[-1]), F32))
    carry = lax.fori_loop(0, i, body, tuple(init for _ in heads))
    r0 = pl.multiple_of(i * blk, blk)
    q_chunk = lax.broadcasted_iota(jnp.int32, (blk, blk), 0) // CHUNK
    k_chunk = lax.broadcasted_iota(jnp.int32, (blk, blk), 1) // CHUNK
    visible = k_chunk <= q_chunk
    for h in heads:
        s = jnp.where(visible, _qk(qs[h], k_ref[h, pl.ds(r0, blk), :]), NEG_INF)
        _, acc = _softmax_step_ones(carry[h], s, v_ref[h, pl.ds(r0, blk), :])
        o_ref[:, h * vd:(h + 1) * vd] = (acc[:, :vd] / acc[:, vd:vd + 1]).astype(o_ref.dtype)


def _prompt_attn(q, k, v, *, vd):
    b, nh, l, dk = q.shape
    blk = min(l, ATTN_BLOCK)
    hp = ATTN_HEADS
    return pl.pallas_call(
        functools.partial(_prompt_attn_kernel, blk=blk, vd=vd),
        grid=(b, nh // hp, l // blk),
        in_specs=[pl.BlockSpec((None, hp, blk, dk), lambda i, h, t: (i, h, t, 0)),
                  pl.BlockSpec((None, hp, l, dk), lambda i, h, t: (i, h, 0, 0)),
                  pl.BlockSpec((None, hp, l, v.shape[-1]), lambda i, h, t: (i, h, 0, 0))],
        out_specs=pl.BlockSpec((None, blk, hp * vd), lambda i, h, t: (i, t, h)),
        out_shape=jax.ShapeDtypeStruct((b, l, nh * vd), BF16),
        compiler_params=_cparams(("parallel", "parallel", "arbitrary")),
        name="prompt_attn",
    )(q, k, v)


def _absorb_q_kernel(q_ref, w_ref, o_ref, *, nope):
    b, lq, dk = q_ref.shape
    q = q_ref[...].reshape(b * lq, dk)[:, :nope]
    o_ref[...] = _bdot(q, w_ref[...]).reshape(b, lq, -1).astype(o_ref.dtype)


def _absorb_q(q, w_ukt, *, nope):
    b, nh, lq, dk = q.shape
    kvl = w_ukt.shape[-1]
    return pl.pallas_call(
        functools.partial(_absorb_q_kernel, nope=nope),
        grid=(nh,),
        in_specs=[pl.BlockSpec((b, None, lq, dk), lambda h: (0, h, 0, 0)),
                  pl.BlockSpec((None, nope, kvl), lambda h: (h, 0, 0))],
        out_specs=pl.BlockSpec((b, None, lq, kvl), lambda h: (0, h, 0, 0)),
        out_shape=jax.ShapeDtypeStruct((b, nh, lq, kvl), BF16),
        compiler_params=_cparams(("parallel",)),
        name="absorb_q",
    )(q, w_ukt)


def _cached_attn_kernel(ql_ref, q_ref, cc_ref, ck_ref, nc_ref, nk_ref, o_ref, *, nope, rope, past, kb):
    nh, lq, kvl = ql_ref.shape
    rows = nh * lq
    ql = ql_ref[...].reshape(rows, kvl)
    qr = q_ref[...].reshape(rows, -1)[:, nope:nope + rope]
    q_pos = past + lax.broadcasted_iota(jnp.int32, (rows, 1), 0) % lq
    carry = (jnp.full((rows, 1), NEG_INF, F32), jnp.zeros((rows, 1), F32), jnp.zeros((rows, kvl), F32))

    def scores(c, kr, k0):
        s = _qk(ql, c) + _qk(qr, kr)
        k_pos = k0 + lax.broadcasted_iota(jnp.int32, s.shape, 1)
        return jnp.where(k_pos // CHUNK <= q_pos // CHUNK, s, NEG_INF)

    for j in range(past // kb):
        c = cc_ref[j * kb:(j + 1) * kb, :].astype(BF16)
        kr = ck_ref[j * kb:(j + 1) * kb, :].astype(BF16)
        carry = _softmax_step(carry, scores(c, kr, j * kb), c)
    c = nc_ref[...].astype(BF16)
    kr = nk_ref[...].astype(BF16)
    _, l, acc = _softmax_step(carry, scores(c, kr, past), c)
    o_ref[...] = (acc / l).reshape(nh, lq, kvl).astype(o_ref.dtype)


def _cached_attn(q_lat, q, cache_ckv, cache_kr, new_ckv, new_kr, *, nope, rope):
    b, nh, lq, kvl = q_lat.shape
    past = cache_ckv.shape[1]
    kb = min(past, CACHE_KEYS)
    dk = q.shape[-1]
    kern = functools.partial(_cached_attn_kernel, nope=nope, rope=rope, past=past, kb=kb)
    return pl.pallas_call(
        kern,
        grid=(b,),
        in_specs=[pl.BlockSpec((None, nh, lq, kvl), lambda i: (i, 0, 0, 0)),
                  pl.BlockSpec((None, nh, lq, dk), lambda i: (i, 0, 0, 0)),
                  pl.BlockSpec((None, past, kvl), lambda i: (i, 0, 0)),
                  pl.BlockSpec((None, past, rope), lambda i: (i, 0, 0)),
                  pl.BlockSpec((None, lq, kvl), lambda i: (i, 0, 0)),
                  pl.BlockSpec((None, lq, rope), lambda i: (i, 0, 0))],
        out_specs=pl.BlockSpec((None, nh, lq, kvl), lambda i: (i, 0, 0, 0)),
        out_shape=jax.ShapeDtypeStruct((b, nh, lq, kvl), BF16),
        compiler_params=_cparams(("parallel",)),
        name="cached_attn",
    )(q_lat, q, cache_ckv, cache_kr, new_ckv, new_kr)


def _absorb_v_kernel(o_ref, w_ref, out_ref):
    b, lq, kvl = o_ref.shape
    out_ref[...] = _bdot(o_ref[...].reshape(b * lq, kvl), w_ref[...]).reshape(b, lq, -1).astype(out_ref.dtype)


def _absorb_v(o_lat, w_uv):
    b, nh, lq, kvl = o_lat.shape
    vd = w_uv.shape[-1]
    return pl.pallas_call(
        _absorb_v_kernel,
        grid=(nh,),
        in_specs=[pl.BlockSpec((b, None, lq, kvl), lambda h: (0, h, 0, 0)),
                  pl.BlockSpec((None, kvl, vd), lambda h: (h, 0, 0))],
        out_specs=pl.BlockSpec((b, lq, vd), lambda h: (0, 0, h)),
        out_shape=jax.ShapeDtypeStruct((b, lq, nh * vd), BF16),
        compiler_params=_cparams(("parallel",)),
        name="absorb_v",
    )(o_lat, w_uv)


def _router_kernel(x_ref, w_ref, o_ref):
    o_ref[...] = _bdot(x_ref[...], w_ref[...])


def _router_logits(x, w):
    t, d = x.shape
    e = w.shape[1]
    tm = 1024
    return pl.pallas_call(
        _router_kernel,
        grid=(pl.cdiv(t, tm),),
        in_specs=[pl.BlockSpec((tm, d), lambda i: (i, 0)), pl.BlockSpec((d, e), lambda i: (0, 0))],
        out_specs=pl.BlockSpec((tm, e), lambda i: (i, 0)),
        out_shape=jax.ShapeDtypeStruct((t, e), F32),
        compiler_params=_cparams(("parallel",)),
        name="router_logits",
    )(x, w)


def _expert_kernel(be_ref, nb_ref, x_ref, wg_ref, wu_ref, wd_ref, o_ref, wgb_ref, wub_ref, wdb_ref):
    i = pl.program_id(0)
    used = i < nb_ref[0]
    new_expert = jnp.logical_or(i == 0, be_ref[i] != be_ref[jnp.maximum(i - 1, 0)])

    @pl.when(jnp.logical_and(used, new_expert))
    def _():
        wgb_ref[...] = wg_ref[...].astype(BF16)
        wub_ref[...] = wu_ref[...].astype(BF16)
        wdb_ref[...] = wd_ref[...].astype(BF16)

    @pl.when(used)
    def _():
        x = x_ref[...]
        g = _bdot(x, wgb_ref[...])
        h = (g * jax.nn.sigmoid(g)) * _bdot(x, wub_ref[...])
        o_ref[...] = _bdot(h.astype(BF16), wdb_ref[...])

    @pl.when(jnp.logical_not(used))
    def _():
        o_ref[...] = jnp.zeros_like(o_ref)


def _expert_mlp(block_e, n_used, xr, w_gate, w_up, w_down, layer):
    n_rows, d = xr.shape
    ff = w_gate.shape[-1]
    bm = MOE_ROWS
    grid_spec = pltpu.PrefetchScalarGridSpec(
        num_scalar_prefetch=2,
        grid=(n_rows // bm,),
        in_specs=[pl.BlockSpec((bm, d), lambda i, be, nb: (i, 0)),
                  pl.BlockSpec((None, None, d, ff), lambda i, be, nb: (layer, be[i], 0, 0)),
                  pl.BlockSpec((None, None, d, ff), lambda i, be, nb: (layer, be[i], 0, 0)),
                  pl.BlockSpec((None, None, ff, d), lambda i, be, nb: (layer, be[i], 0, 0))],
        out_specs=pl.BlockSpec((bm, d), lambda i, be, nb: (i, 0)),
        scratch_shapes=[pltpu.VMEM((d, ff), BF16), pltpu.VMEM((d, ff), BF16), pltpu.VMEM((ff, d), BF16)],
    )
    return pl.pallas_call(
        _expert_kernel,
        grid_spec=grid_spec,
        out_shape=jax.ShapeDtypeStruct((n_rows, d), F32),
        compiler_params=_cparams(("arbitrary",)),
        name="expert_mlp",
    )(block_e, n_used, xr, w_gate, w_up, w_down)


def _first_argmax(x):
    return jnp.argmax(x, axis=-1).astype(jnp.int32)


def _moe(h, w_router, router_bias, w_gate, w_up, w_down, layer):
    t, d = h.shape
    ne = w_router.shape[1]
    ng = N_EXPERT_GROUPS
    eg = ne // ng
    scores = jax.nn.sigmoid(_router_logits(h, w_router))
    sel = scores + router_bias
    grp = sel.reshape(t, ng, eg)
    lane = jnp.arange(eg, dtype=jnp.int32)
    i1 = _first_argmax(grp)
    m1 = jnp.max(grp, axis=-1)
    m2 = jnp.max(jnp.where(lane == i1[..., None], -jnp.inf, grp), axis=-1)
    g_idx = _first_argmax(m1 + m2)
    sel_g = jnp.take_along_axis(grp, g_idx[:, None, None], axis=1)[:, 0]
    sc_g = jnp.take_along_axis(scores.reshape(t, ng, eg), g_idx[:, None, None], axis=1)[:, 0]
    e1 = _first_argmax(sel_g)
    e2 = _first_argmax(jnp.where(lane == e1[:, None], -jnp.inf, sel_g))
    w1 = jnp.take_along_axis(sc_g, e1[:, None], axis=1)[:, 0]
    w2 = jnp.take_along_axis(sc_g, e2[:, None], axis=1)[:, 0]
    wsum = w1 + w2
    e_idx = jnp.stack([g_idx * eg + e1, g_idx * eg + e2], axis=1)
    w = jnp.stack([w1 / wsum, w2 / wsum], axis=1)

    bm = MOE_ROWS
    n_assign = t * TOP_K
    flat_e = e_idx.reshape(-1)
    order = jnp.argsort(flat_e).astype(jnp.int32)
    rank = jnp.argsort(order).astype(jnp.int32)
    counts = jnp.sum((flat_e[:, None] == jnp.arange(ne, dtype=jnp.int32)[None, :]).astype(jnp.int32), axis=0)
    starts = jnp.cumsum(counts) - counts
    pcounts = (counts + bm - 1) // bm * bm
    pends = jnp.cumsum(pcounts)
    pstarts = pends - pcounts
    dest = ((pstarts - starts)[flat_e] + rank).reshape(t, TOP_K)
    n_blocks = -(-n_assign // bm) + ne
    blk_start = jnp.arange(n_blocks, dtype=jnp.int32) * bm
    block_e = jnp.minimum(jnp.sum((pends[None, :] <= blk_start[:, None]).astype(jnp.int32), axis=1), ne - 1)
    n_used = (pends[ne - 1:] // bm).astype(jnp.int32)
    off = (blk_start - pstarts[block_e])[:, None] + jnp.arange(bm, dtype=jnp.int32)[None, :]
    valid = off < counts[block_e][:, None]
    src = jnp.clip(starts[block_e][:, None] + off, 0, n_assign - 1)
    row_tok = jnp.where(valid, order[src.reshape(-1)].reshape(n_blocks, bm) // TOP_K, 0).reshape(-1)
    yr = _expert_mlp(block_e, n_used, h[row_tok], w_gate, w_up, w_down, layer)
    return w[:, 0:1] * yr[dest[:, 0]] + w[:, 1:2] * yr[dest[:, 1]]


def _rope_tables(pos, rope):
    inv = 1.0 / (ROPE_THETA ** (jnp.arange(0, rope, 2, dtype=F32) / rope))
    ang = pos.astype(F32)[:, None] * inv[None, :]
    cos, sin = jnp.cos(ang), jnp.sin(ang)
    pad = jnp.zeros((pos.shape[0], LANES - rope), F32)
    return jnp.concatenate([cos, cos, pad], axis=1), jnp.concatenate([-sin, sin, pad], axis=1)


def kernel(x_prompt, x_sample, c_prompt, c_sample, state_ssm_re, state_ssm_im, cache_ckv, cache_krope, w_ada, b_ada, ln_g, ln_b, ssm_a_re, ssm_a_im, ssm_log_dt, ssm_b_re, ssm_b_im, ssm_c_re, ssm_c_im, ssm_d, ssm_w_glu, mla_w_in, mla_q_norm, mla_kv_norm, mla_w_uq, mla_w_uk, mla_w_uv, mla_w_o, w_router, router_bias, moe_w_gate, moe_w_up, moe_w_down):
    depth, d = w_ada.shape[0], w_ada.shape[1]
    alpha = (2 * depth) ** 0.25
    nh, nope, vd = mla_w_uk.shape[2], mla_w_uk.shape[3], mla_w_uv.shape[3]
    ql, kvl = mla_q_norm.shape[1], mla_kv_norm.shape[1]
    rope = mla_w_uq.shape[3] - nope
    assert nope == LANES and vd == LANES and rope <= LANES
    scale = (nope + rope) ** -0.5 * math.log2(math.e)
    n_groups, n_state = ssm_a_re.shape[1], ssm_a_re.shape[2]
    past = cache_ckv.shape[2]

    xs = [x_prompt, x_sample]
    bsz = [x.shape[0] for x in xs]
    seq = [x.shape[1] for x in xs]
    row0 = [0, bsz[0] * seq[0]]
    pos = [jnp.arange(seq[0]), past + jnp.arange(seq[1])]
    tabs = [_rope_tables(p, rope) for p in pos]

    mod = _ada_mod(jnp.concatenate([c_prompt, c_sample], axis=0), w_ada, b_ada)

    def mod_rows(i, grp, piece):
        b0 = 0 if grp == 0 else bsz[0]
        return mod[i, b0:b0 + bsz[grp], piece * d:(piece + 1) * d].reshape(bsz[grp], 1, d)

    assert depth % 2 == 0, "layers alternate S5 / MLA mixers, starting with S5"
    w_router_b = w_router.astype(BF16)
    half = rope // 2
    ssm_out = [[[], []], [[], []]]
    ckv_out, kr_out = [[], []], [[], []]

    h_in = [None, None]
    for i in range(depth):
        j = i // 2
        x1s, h2s = [], []
        if i % 2 == 0:
            bd, cd, ab = _s5_tables(ssm_a_re[j], ssm_a_im[j], ssm_log_dt[j], ssm_b_re[j], ssm_b_im[j],
                                    ssm_c_re[j], ssm_c_im[j])
            w_glu = ssm_w_glu[j].astype(BF16)
        else:
            w_in = mla_w_in[j]
            r0 = ql + kvl
            zc = jnp.zeros((d, LANES - rope), F32)
            w_in_ext = jnp.concatenate([w_in, zc, w_in[:, r0 + half:], w_in[:, r0:r0 + half], zc], axis=1).astype(BF16)
            wq = mla_w_uq[j]
            zq = jnp.zeros((ql, nh, LANES - rope), F32)
            wq_rope = jnp.concatenate([wq[..., nope:], zq], axis=-1)
            wq_rot = jnp.concatenate([wq[..., nope + half:], wq[..., nope:nope + half], zq], axis=-1)
            wq = jnp.concatenate([wq[..., :nope].reshape(ql, -1), wq_rope.reshape(ql, -1),
                                  wq_rot.reshape(ql, -1)], axis=1).astype(BF16)
            wkv = jnp.concatenate([mla_w_uk[j].reshape(kvl, -1), mla_w_uv[j].reshape(kvl, -1)],
                                  axis=1).astype(BF16)
            w_ukt = mla_w_uk[j].transpose(1, 2, 0).astype(BF16)
            w_uvh = mla_w_uv[j].transpose(1, 0, 2).astype(BF16)
            w_o = mla_w_o[j].astype(BF16)
        for grp in range(2):
            x = xs[grp]
            b, l = bsz[grp], seq[grp]
            sh1, sc1, g1, sh2, sc2 = (mod_rows(i, grp, p) for p in range(5))
            lng, lnb = ln_g[i, 0].reshape(1, d), ln_b[i, 0].reshape(1, d)
            if i % 2 == 0:
                u = _modulate_to_s5_layout(x, sc1, sh1)
                nbg = b // SUBLANES
                if grp == 0:
                    h0r = jnp.zeros((nbg, SUBLANES, n_groups * n_state), F32)
                    h0i = h0r
                else:
                    h0r = state_ssm_re[j].reshape(nbg, SUBLANES, n_groups * n_state)
                    h0i = state_ssm_im[j].reshape(nbg, SUBLANES, n_groups * n_state)
                z, hr, hi = _s5_scan(u, bd, cd, ab, ssm_d[j].reshape(1, d), h0r, h0i)
                ssm_out[grp][0].append(hr.reshape(b, n_groups, n_state))
                ssm_out[grp][1].append(hi.reshape(b, n_groups, n_state))
                z = z.reshape(nbg, l, SUBLANES * d)

                def a_spec(tm):
                    return pl.BlockSpec((None, tm, d), lambda bi, t: (bi // SUBLANES, t, bi % SUBLANES))

                x1, h2 = _mix_ln(z, a_spec, w_glu, x, g1, lng, lnb, sc2, sh2, glu=True, alpha=alpha)
            else:
                cos_t, sin_t = tabs[grp]
                cq, ckv, kr, krp = _mla_in(h_in[grp], w_in_ext, mla_q_norm[j].reshape(1, ql),
                                           mla_kv_norm[j].reshape(1, kvl), cos_t, sin_t, ql=ql, kvl=kvl, rope=rope)
                q = _mla_q(cq, wq, cos_t, sin_t, nh=nh, scale=scale)
                if grp == 0:
                    k, v = _mla_kv(ckv, krp, wkv, nh=nh)
                    o = _prompt_attn(q, k, v, vd=vd)
                else:
                    q_lat = _absorb_q(q, w_ukt, nope=nope)
                    o_lat = _cached_attn(q_lat, q, cache_ckv[j], cache_krope[j], ckv, kr, nope=nope, rope=rope)
                    o = _absorb_v(o_lat, w_uvh)
                ckv_out[grp].append(ckv)
                kr_out[grp].append(kr)

                def a_spec(tm):
                    return pl.BlockSpec((None, tm, nh * vd), lambda bi, t: (bi, t, 0))

                x1, h2 = _mix_ln(o, a_spec, w_o, x, g1, lng, lnb, sc2, sh2, glu=False, alpha=alpha)
            x1s.append(x1)
            h2s.append(h2)
        h2_all = jnp.concatenate([h.reshape(-1, d) for h in h2s], axis=0)
        f_all = _moe(h2_all, w_router_b, router_bias, moe_w_gate, moe_w_up, moe_w_down, i)
        for grp in range(2):
            g2 = mod_rows(i, grp, 5)
            next_mod = (mod_rows(i + 1, grp, 1), mod_rows(i + 1, grp, 0)) if i % 2 == 0 else None
            xs[grp], h_in[grp] = _ffn_ln(f_all, row0[grp], x1s[grp], g2, ln_g[i, 1].reshape(1, d),
                                         ln_b[i, 1].reshape(1, d), next_mod, alpha=alpha)

    return (xs[0], xs[1],
            jnp.stack(ssm_out[0][0]), jnp.stack(ssm_out[0][1]), jnp.stack(ckv_out[0]), jnp.stack(kr_out[0]),
            jnp.stack(ssm_out[1][0]), jnp.stack(ssm_out[1][1]), jnp.stack(ckv_out[1]), jnp.stack(kr_out[1]))
```

```python
import functools
import math

import jax
import jax.numpy as jnp
from jax import lax
from jax.experimental import pallas as pl
from jax.experimental.pallas import tpu as pltpu

F32 = jnp.float32
BF16 = jnp.bfloat16

CHUNK = 64
SSM_GROUP_WIDTH = 16
N_EXPERT_GROUPS = 8
TOP_K = 2
ROPE_THETA = 10000.0
LN_EPS = 1e-5
RMS_EPS = 1e-6
NEG_INF = -1e30

V7X_VMEM_BYTES = 64 * 1024 * 1024
SUBLANES = 8
LANES = 128
MXU_DIM = 256

VMEM_LIMIT = 56 * 1024 * 1024
S5_CHANNELS = MXU_DIM
S5_TIME = 128
ROW_TILE = 256
ATTN_BLOCK = 512
ATTN_HEADS = 2
MOE_ROWS = 256
CACHE_KEYS = 1024


def _cparams(sem):
    return pltpu.CompilerParams(dimension_semantics=sem, vmem_limit_bytes=VMEM_LIMIT)


def _bdot(a, b):
    return jnp.dot(a, b, preferred_element_type=F32)


def _ada_kernel(c_ref, w_ref, b_ref, o_ref):
    c = c_ref[...]
    cs = (c * jax.nn.sigmoid(c)).astype(BF16)
    o_ref[...] = _bdot(cs, w_ref[...].astype(BF16)) + b_ref[...]


def _ada_mod(c_all, w_ada, b_ada):
    depth, d, n = w_ada.shape
    bc = c_all.shape[0]
    tn = 1024
    return pl.pallas_call(
        _ada_kernel,
        grid=(depth, n // tn),
        in_specs=[pl.BlockSpec((bc, d), lambda i, j: (0, 0)),
                  pl.BlockSpec((None, d, tn), lambda i, j: (i, 0, j)),
                  pl.BlockSpec((None, 1, tn), lambda i, j: (i, 0, j))],
        out_specs=pl.BlockSpec((None, bc, tn), lambda i, j: (i, 0, j)),
        out_shape=jax.ShapeDtypeStruct((depth, bc, n), F32),
        compiler_params=_cparams(("parallel", "parallel")),
        name="ada_mod",
    )(c_all, w_ada, b_ada.reshape(depth, 1, n))


def _modulate_kernel(x_ref, sc_ref, sh_ref, o_ref):
    o_ref[...] = x_ref[...] * (1.0 + sc_ref[...]) + sh_ref[...]


def _modulate_to_s5_layout(x, sc, sh):
    b, l, d = x.shape
    tl = min(l, 512)
    out = pl.pallas_call(
        _modulate_kernel,
        grid=(b, l // tl),
        in_specs=[pl.BlockSpec((None, tl, d), lambda i, t: (i, t, 0)),
                  pl.BlockSpec((None, 1, d), lambda i, t: (i, 0, 0)),
                  pl.BlockSpec((None, 1, d), lambda i, t: (i, 0, 0))],
        out_specs=pl.BlockSpec((None, tl, d), lambda i, t: (i // SUBLANES, t, i % SUBLANES)),
        out_shape=jax.ShapeDtypeStruct((b // SUBLANES, l, SUBLANES * d), F32),
        compiler_params=_cparams(("parallel", "parallel")),
        name="modulate_s5",
    )(x, sc, sh)
    return out.reshape(b // SUBLANES, l, SUBLANES, d)


def _s5_kernel(u_ref, bd_ref, cd_ref, ab_ref, d_ref, h0r_ref, h0i_ref,
               z_ref, hr_ref, hi_ref, xs_ref, h_ref, *, tt, sb):
    t = pl.program_id(2)

    @pl.when(t == 0)
    def _():
        h_ref[0] = h0r_ref[...]
        h_ref[1] = h0i_ref[...]

    cb = u_ref.shape[-1]
    u = u_ref[...].reshape(tt * SUBLANES, cb)
    xs_ref[...] = _bdot(u.astype(BF16), bd_ref[...])
    a_re = jnp.broadcast_to(ab_ref[0:1, :], (SUBLANES, sb))
    a_im = jnp.broadcast_to(ab_ref[1:2, :], (SUBLANES, sb))

    def step(i, carry):
        h_re, h_im = carry
        r0 = pl.multiple_of(i * SUBLANES, SUBLANES)
        x_re = xs_ref[pl.ds(r0, SUBLANES), 0:sb]
        x_im = xs_ref[pl.ds(r0, SUBLANES), sb:2 * sb]
        n_re = a_re * h_re - a_im * h_im + x_re
        n_im = a_re * h_im + a_im * h_re + x_im
        xs_ref[pl.ds(r0, SUBLANES), 0:sb] = n_re
        xs_ref[pl.ds(r0, SUBLANES), sb:2 * sb] = n_im
        return n_re, n_im

    h_re, h_im = lax.fori_loop(0, tt, step, (h_ref[0], h_ref[1]))
    h_ref[0] = h_re
    h_ref[1] = h_im
    y = _bdot(xs_ref[...].astype(BF16), cd_ref[...]) + d_ref[...] * u
    z_ref[...] = jax.nn.gelu(y).astype(z_ref.dtype)

    @pl.when(t == pl.num_programs(2) - 1)
    def _():
        hr_ref[...] = h_re
        hi_ref[...] = h_im


def _s5_scan(u, bd, cd, ab, d_skip, h0_re, h0_im):
    nbg, l, _, d = u.shape
    nj, cb, sb2 = bd.shape
    sb = sb2 // 2
    tt = min(l, S5_TIME)
    nt = l // tt
    kern = functools.partial(_s5_kernel, tt=tt, sb=sb)
    state_spec = pl.BlockSpec((None, SUBLANES, sb), lambda g, j, t: (g, 0, j))
    state_shape = jax.ShapeDtypeStruct((nbg, SUBLANES, nj * sb), F32)
    return pl.pallas_call(
        kern,
        grid=(nbg, nj, nt),
        in_specs=[pl.BlockSpec((None, tt, SUBLANES, cb), lambda g, j, t: (g, t, 0, j)),
                  pl.BlockSpec((None, cb, sb2), lambda g, j, t: (j, 0, 0)),
                  pl.BlockSpec((None, sb2, cb), lambda g, j, t: (j, 0, 0)),
                  pl.BlockSpec((None, 2, sb), lambda g, j, t: (j, 0, 0)),
                  pl.BlockSpec((1, cb), lambda g, j, t: (0, j)),
                  state_spec, state_spec],
        out_specs=[pl.BlockSpec((tt * SUBLANES, cb), lambda g, j, t: (g * nt + t, j)),
                   state_spec, state_spec],
        out_shape=[jax.ShapeDtypeStruct((nbg * l * SUBLANES, d), BF16), state_shape, state_shape],
        scratch_shapes=[pltpu.VMEM((tt * SUBLANES, sb2), F32), pltpu.VMEM((2, SUBLANES, sb), F32)],
        compiler_params=_cparams(("parallel", "parallel", "arbitrary")),
        name="s5_scan",
    )(u, bd, cd, ab, d_skip, h0_re, h0_im)


def _s5_tables(a_re, a_im, log_dt, b_re, b_im, c_re, c_im):
    g, p = a_re.shape
    w = b_re.shape[-1]
    gl = S5_CHANNELS // w
    nj = g // gl
    dt = jnp.exp(log_dt)[:, None]
    mag = jnp.exp(a_re * dt)
    abar_re, abar_im = mag * jnp.cos(a_im * dt), mag * jnp.sin(a_im * dt)
    den = a_re * a_re + a_im * a_im
    n_re, n_im = abar_re - 1.0, abar_im
    f_re = (n_re * a_re + n_im * a_im) / den
    f_im = (n_im * a_re - n_re * a_im) / den
    bb_re = f_re[..., None] * b_re - f_im[..., None] * b_im
    bb_im = f_re[..., None] * b_im + f_im[..., None] * b_re
    eye = jnp.eye(gl, dtype=F32)
    bb = jnp.stack([bb_re, bb_im]).reshape(2, nj, gl, p, w)
    bd = jnp.einsum('sjgpw,gh->jgwshp', bb, eye).reshape(nj, gl * w, 2 * gl * p)
    cc = jnp.stack([c_re, -c_im]).reshape(2, nj, gl, w, p)
    cd = jnp.einsum('sjgwp,gh->jsgphw', cc, eye).reshape(nj, 2 * gl * p, gl * w)
    ab = jnp.stack([abar_re.reshape(nj, gl * p), abar_im.reshape(nj, gl * p)], axis=1)
    return bd.astype(BF16), cd.astype(BF16), ab


def _layer_norm(y, g, b):
    mu = jnp.mean(y, axis=-1, keepdims=True)
    yc = y - mu
    var = jnp.mean(yc * yc, axis=-1, keepdims=True)
    return yc * lax.rsqrt(var + LN_EPS) * g + b


def _mix_ln_kernel(a_ref, w_ref, x_ref, gate_ref, lng_ref, lnb_ref, sc_ref, sh_ref, tokens_hbm,
                   x1_ref, h_ref, *, glu, alpha):
    del tokens_hbm
    acc = _bdot(a_ref[...], w_ref[...])
    if glu:
        d = acc.shape[-1] // 2
        m = acc[:, :d] * jax.nn.sigmoid(acc[:, d:])
    else:
        m = acc
    x1 = _layer_norm(alpha * x_ref[...] + gate_ref[...] * m, lng_ref[...], lnb_ref[...])
    x1_ref[...] = x1
    h_ref[...] = x1 * (1.0 + sc_ref[...]) + sh_ref[...]


def _mix_ln(a, a_spec, w, x, gate, ln_g, ln_b, sc, sh, tokens, row0, *, glu, alpha):
    b, l, d = x.shape
    tm = min(l, ROW_TILE)
    nt = l // tm
    blk0 = row0 // tm
    k, n = w.shape
    row = pl.BlockSpec((None, 1, d), lambda i, t: (i, 0, 0))
    vec = pl.BlockSpec((1, d), lambda i, t: (0, 0))
    tile = pl.BlockSpec((None, tm, d), lambda i, t: (i, t, 0))
    return pl.pallas_call(
        functools.partial(_mix_ln_kernel, glu=glu, alpha=alpha),
        grid=(b, nt),
        in_specs=[a_spec(tm), pl.BlockSpec((k, n), lambda i, t: (0, 0), pipeline_mode=pl.Buffered(1)),
                  tile, row, vec, vec, row, row, pl.BlockSpec(memory_space=pl.ANY)],
        out_specs=[tile, pl.BlockSpec((tm, d), lambda i, t: (blk0 + i * nt + t, 0))],
        out_shape=[jax.ShapeDtypeStruct((b, l, d), F32), jax.ShapeDtypeStruct(tokens.shape, F32)],
        input_output_aliases={8: 1},
        compiler_params=_cparams(("parallel", "parallel")),
        name="mix_ln_glu" if glu else "mix_ln_proj",
    )(a, w, x, gate, ln_g, ln_b, sc, sh, tokens)


def _ffn_ln_kernel(*refs, alpha):
    y_refs, (wt_ref, x_ref, gate_ref, lng_ref, lnb_ref), rest = refs[:TOP_K], refs[TOP_K:TOP_K + 5], refs[TOP_K + 5:]
    f = wt_ref[:, 0:1] * y_refs[0][...]
    for k in range(1, TOP_K):
        f = f + wt_ref[:, k:k + 1] * y_refs[k][...]
    x2 = _layer_norm(alpha * x_ref[...] + gate_ref[...] * f, lng_ref[...], lnb_ref[...])
    if len(rest) == 1:
        rest[0][...] = x2
    else:
        sc_ref, sh_ref, x2_ref, h_ref = rest
        x2_ref[...] = x2
        h_ref[...] = (x2 * (1.0 + sc_ref[...]) + sh_ref[...]).astype(h_ref.dtype)


def _ffn_ln(y_all, wt_all, row0, x, gate, ln_g, ln_b, next_mod, *, alpha):
    b, l, d = x.shape
    tm = min(l, ROW_TILE)
    nt = l // tm
    blk0 = row0 // tm
    kblk = wt_all.shape[0] // tm
    row = pl.BlockSpec((None, 1, d), lambda i, t: (i, 0, 0))
    vec = pl.BlockSpec((1, d), lambda i, t: (0, 0))
    tile = pl.BlockSpec((None, tm, d), lambda i, t: (i, t, 0))
    in_specs = [pl.BlockSpec((tm, d), functools.partial(lambda k, i, t: (k * kblk + blk0 + i * nt + t, 0), k))
                for k in range(TOP_K)]
    in_specs += [pl.BlockSpec((tm, TOP_K), lambda i, t: (blk0 + i * nt + t, 0)), tile, row, vec, vec]
    x2_shape = jax.ShapeDtypeStruct((b, l, d), F32)
    if next_mod is None:
        args, out_specs, out_shape = (), tile, x2_shape
    else:
        args, in_specs = tuple(next_mod), in_specs + [row, row]
        out_specs, out_shape = [tile, tile], [x2_shape, jax.ShapeDtypeStruct((b, l, d), BF16)]
    out = pl.pallas_call(
        functools.partial(_ffn_ln_kernel, alpha=alpha),
        grid=(b, nt),
        in_specs=in_specs,
        out_specs=out_specs,
        out_shape=out_shape,
        compiler_params=_cparams(("parallel", "parallel")),
        name="ffn_ln",
    )(*([y_all] * TOP_K), wt_all, x, gate, ln_g, ln_b, *args)
    return (out, None) if next_mod is None else out


def _rms_norm(x, g):
    ms = jnp.mean(x * x, axis=-1, keepdims=True)
    return x * lax.rsqrt(ms + RMS_EPS) * g


def _mla_in_kernel(h_ref, w_ref, qn_ref, kvn_ref, cos_ref, sin_ref,
                   cq_ref, ckv_ref, kr_ref, krp_ref, *, ql, kvl, rope):
    acc = _bdot(h_ref[...], w_ref[...])
    cq_ref[...] = _rms_norm(acc[:, :ql], qn_ref[...]).astype(cq_ref.dtype)
    ckv_ref[...] = _rms_norm(acc[:, ql:ql + kvl], kvn_ref[...])
    o = ql + kvl
    kr = acc[:, o:o + LANES] * cos_ref[...] + acc[:, o + LANES:o + 2 * LANES] * sin_ref[...]
    kr_ref[...] = kr[:, :rope]
    krp_ref[...] = kr.astype(krp_ref.dtype)


def _mla_in(h, w_in_ext, q_norm, kv_norm, cos_t, sin_t, *, ql, kvl, rope):
    b, l, d = h.shape
    tm = min(l, ROW_TILE)
    n = w_in_ext.shape[1]
    kern = functools.partial(_mla_in_kernel, ql=ql, kvl=kvl, rope=rope)
    tab = pl.BlockSpec((tm, LANES), lambda i, t: (t, 0))

    def out(nf):
        return pl.BlockSpec((None, tm, nf), lambda i, t: (i, t, 0))

    return pl.pallas_call(
        kern,
        grid=(b, l // tm),
        in_specs=[pl.BlockSpec((None, tm, d), lambda i, t: (i, t, 0)),
                  pl.BlockSpec((d, n), lambda i, t: (0, 0)),
                  pl.BlockSpec((1, ql), lambda i, t: (0, 0)),
                  pl.BlockSpec((1, kvl), lambda i, t: (0, 0)), tab, tab],
        out_specs=[out(ql), out(kvl), out(rope), out(LANES)],
        out_shape=[jax.ShapeDtypeStruct((b, l, ql), BF16), jax.ShapeDtypeStruct((b, l, kvl), F32),
                   jax.ShapeDtypeStruct((b, l, rope), F32), jax.ShapeDtypeStruct((b, l, LANES), BF16)],
        compiler_params=_cparams(("parallel", "parallel")),
        name="mla_in",
    )(h, w_in_ext, q_norm, kv_norm, cos_t, sin_t)


def _mla_q_kernel(cq_ref, w_ref, cos_ref, sin_ref, q_ref, *, nh, scale):
    acc = _bdot(cq_ref[...], w_ref[...])
    cos = cos_ref[...] * scale
    sin = sin_ref[...] * scale
    for h in range(nh):
        q_ref[h, :, :LANES] = (acc[:, h * LANES:(h + 1) * LANES] * scale).astype(q_ref.dtype)
        r0, p0 = (nh + h) * LANES, (2 * nh + h) * LANES
        q_ref[h, :, LANES:] = (acc[:, r0:r0 + LANES] * cos + acc[:, p0:p0 + LANES] * sin).astype(q_ref.dtype)


def _mla_q(cq, wq, cos_t, sin_t, *, nh, scale):
    b, l, ql = cq.shape
    n = wq.shape[1]
    tm = min(l, ROW_TILE)
    tab = pl.BlockSpec((tm, LANES), lambda i, t: (t, 0))
    return pl.pallas_call(
        functools.partial(_mla_q_kernel, nh=nh, scale=scale),
        grid=(b, l // tm),
        in_specs=[pl.BlockSpec((None, tm, ql), lambda i, t: (i, t, 0)),
                  pl.BlockSpec((ql, n), lambda i, t: (0, 0)), tab, tab],
        out_specs=pl.BlockSpec((None, nh, tm, 2 * LANES), lambda i, t: (i, 0, t, 0)),
        out_shape=jax.ShapeDtypeStruct((b, nh, l, 2 * LANES), BF16),
        compiler_params=_cparams(("parallel", "parallel")),
        name="mla_q",
    )(cq, wq, cos_t, sin_t)


def _mla_kv_kernel(ckv_ref, krp_ref, w_ref, k_ref, v_ref, *, nh):
    acc = _bdot(ckv_ref[...].astype(BF16), w_ref[...])
    krp = krp_ref[...]
    lane = lax.broadcasted_iota(jnp.int32, krp.shape, 1)
    ones_col = jnp.where(lane == 0, 1.0, 0.0).astype(v_ref.dtype)
    for h in range(nh):
        k_ref[h, :, :LANES] = acc[:, h * LANES:(h + 1) * LANES].astype(k_ref.dtype)
        k_ref[h, :, LANES:] = krp
        v_ref[h, :, :LANES] = acc[:, (nh + h) * LANES:(nh + h + 1) * LANES].astype(v_ref.dtype)
        v_ref[h, :, LANES:] = ones_col


def _mla_kv(ckv, krp, wkv, *, nh):
    b, l, kvl = ckv.shape
    n = wkv.shape[1]
    tm = min(l, ROW_TILE)
    head_tile = pl.BlockSpec((None, nh, tm, 2 * LANES), lambda i, t: (i, 0, t, 0))
    head_shape = jax.ShapeDtypeStruct((b, nh, l, 2 * LANES), BF16)
    return pl.pallas_call(
        functools.partial(_mla_kv_kernel, nh=nh),
        grid=(b, l // tm),
        in_specs=[pl.BlockSpec((None, tm, kvl), lambda i, t: (i, t, 0)),
                  pl.BlockSpec((None, tm, LANES), lambda i, t: (i, t, 0)),
                  pl.BlockSpec((kvl, n), lambda i, t: (0, 0))],
        out_specs=[head_tile, head_tile],
        out_shape=[head_shape, head_shape],
        compiler_params=_cparams(("parallel", "parallel")),
        name="mla_kv",
    )(ckv, krp, wkv)


def _qk(q, k):
    return lax.dot_general(q, k, (((1,), (1,)), ((), ())), preferred_element_type=F32)


def _softmax_step(carry, s, v):
    m, l, acc = carry
    m_new = jnp.maximum(m, jnp.max(s, axis=-1, keepdims=True))
    p = jnp.exp2(s - m_new)
    alpha = jnp.exp2(m - m_new)
    l = alpha * l + jnp.sum(p, axis=-1, keepdims=True)
    acc = alpha * acc + _bdot(p.astype(BF16), v)
    return m_new, l, acc


def _softmax_step_ones(carry, s, v):
    m, acc = carry
    m_new = jnp.maximum(m, jnp.max(s, axis=-1, keepdims=True))
    p = jnp.exp2(s - m_new)
    acc = jnp.exp2(m - m_new) * acc + _bdot(p.astype(BF16), v)
    return m_new, acc


def _prompt_attn_kernel(q_ref, k_ref, v_ref, o_ref, *, blk, vd):
    i = pl.program_id(2)
    heads = range(q_ref.shape[0])
    qs = [q_ref[h] for h in heads]

    def body(j, carry):
        r0 = pl.multiple_of(j * blk, blk)
        return tuple(_softmax_step_ones(carry[h], _qk(qs[h], k_ref[h, pl.ds(r0, blk), :]),
                                        v_ref[h, pl.ds(r0, blk), :]) for h in heads)

    init = (jnp.full((blk, 1), NEG_INF, F32), jnp.zeros((blk, v_ref.shape[-1]), F32))
    carry = lax.fori_loop(0, i, body, tuple(init for _ in heads))
    r0 = pl.multiple_of(i * blk, blk)
    q_chunk = lax.broadcasted_iota(jnp.int32, (blk, blk), 0) // CHUNK
    k_chunk = lax.broadcasted_iota(jnp.int32, (blk, blk), 1) // CHUNK
    visible = k_chunk <= q_chunk
    for h in heads:
        s = jnp.where(visible, _qk(qs[h], k_ref[h, pl.ds(r0, blk), :]), NEG_INF)
        _, acc = _softmax_step_ones(carry[h], s, v_ref[h, pl.ds(r0, blk), :])
        o_ref[:, h * vd:(h + 1) * vd] = (acc[:, :vd] / acc[:, vd:vd + 1]).astype(o_ref.dtype)


def _prompt_attn(q, k, v, *, vd):
    b, nh, l, dk = q.shape
    blk = min(l, ATTN_BLOCK)
    hp = ATTN_HEADS
    return pl.pallas_call(
        functools.partial(_prompt_attn_kernel, blk=blk, vd=vd),
        grid=(b, nh // hp, l // blk),
        in_specs=[pl.BlockSpec((None, hp, blk, dk), lambda i, h, t: (i, h, t, 0)),
                  pl.BlockSpec((None, hp, l, dk), lambda i, h, t: (i, h, 0, 0)),
                  pl.BlockSpec((None, hp, l, v.shape[-1]), lambda i, h, t: (i, h, 0, 0))],
        out_specs=pl.BlockSpec((None, blk, hp * vd), lambda i, h, t: (i, t, h)),
        out_shape=jax.ShapeDtypeStruct((b, l, nh * vd), BF16),
        compiler_params=_cparams(("parallel", "parallel", "arbitrary")),
        name="prompt_attn",
    )(q, k, v)


def _absorb_q_kernel(q_ref, w_ref, o_ref, *, nope):
    b, lq, dk = q_ref.shape
    q = q_ref[...].reshape(b * lq, dk)[:, :nope]
    o_ref[...] = _bdot(q, w_ref[...]).reshape(b, lq, -1).astype(o_ref.dtype)


def _absorb_q(q, w_ukt, *, nope):
    b, nh, lq, dk = q.shape
    kvl = w_ukt.shape[-1]
    return pl.pallas_call(
        functools.partial(_absorb_q_kernel, nope=nope),
        grid=(nh,),
        in_specs=[pl.BlockSpec((b, None, lq, dk), lambda h: (0, h, 0, 0)),
                  pl.BlockSpec((None, nope, kvl), lambda h: (h, 0, 0))],
        out_specs=pl.BlockSpec((b, None, lq, kvl), lambda h: (0, h, 0, 0)),
        out_shape=jax.ShapeDtypeStruct((b, nh, lq, kvl), BF16),
        compiler_params=_cparams(("parallel",)),
        name="absorb_q",
    )(q, w_ukt)


def _cached_attn_kernel(ql_ref, q_ref, cc_ref, ck_ref, nc_ref, nk_ref, o_ref, *, nope, rope, past, kb):
    nh, lq, kvl = ql_ref.shape
    rows = nh * lq
    ql = ql_ref[...].reshape(rows, kvl)
    qr = q_ref[...].reshape(rows, -1)[:, nope:nope + rope]
    q_pos = past + lax.broadcasted_iota(jnp.int32, (rows, 1), 0) % lq
    carry = (jnp.full((rows, 1), NEG_INF, F32), jnp.zeros((rows, 1), F32), jnp.zeros((rows, kvl), F32))

    def scores(c, kr, k0):
        s = _qk(ql, c) + _qk(qr, kr)
        k_pos = k0 + lax.broadcasted_iota(jnp.int32, s.shape, 1)
        return jnp.where(k_pos // CHUNK <= q_pos // CHUNK, s, NEG_INF)

    for j in range(past // kb):
        c = cc_ref[j * kb:(j + 1) * kb, :].astype(BF16)
        kr = ck_ref[j * kb:(j + 1) * kb, :].astype(BF16)
        carry = _softmax_step(carry, scores(c, kr, j * kb), c)
    c = nc_ref[...].astype(BF16)
    kr = nk_ref[...].astype(BF16)
    _, l, acc = _softmax_step(carry, scores(c, kr, past), c)
    o_ref[...] = (acc / l).reshape(nh, lq, kvl).astype(o_ref.dtype)


def _cached_attn(q_lat, q, cache_ckv, cache_kr, new_ckv, new_kr, *, nope, rope):
    b, nh, lq, kvl = q_lat.shape
    past = cache_ckv.shape[1]
    kb = min(past, CACHE_KEYS)
    dk = q.shape[-1]
    kern = functools.partial(_cached_attn_kernel, nope=nope, rope=rope, past=past, kb=kb)
    return pl.pallas_call(
        kern,
        grid=(b,),
        in_specs=[pl.BlockSpec((None, nh, lq, kvl), lambda i: (i, 0, 0, 0)),
                  pl.BlockSpec((None, nh, lq, dk), lambda i: (i, 0, 0, 0)),
                  pl.BlockSpec((None, past, kvl), lambda i: (i, 0, 0)),
                  pl.BlockSpec((None, past, rope), lambda i: (i, 0, 0)),
                  pl.BlockSpec((None, lq, kvl), lambda i: (i, 0, 0)),
                  pl.BlockSpec((None, lq, rope), lambda i: (i, 0, 0))],
        out_specs=pl.BlockSpec((None, nh, lq, kvl), lambda i: (i, 0, 0, 0)),
        out_shape=jax.ShapeDtypeStruct((b, nh, lq, kvl), BF16),
        compiler_params=_cparams(("parallel",)),
        name="cached_attn",
    )(q_lat, q, cache_ckv, cache_kr, new_ckv, new_kr)


def _absorb_v_kernel(o_ref, w_ref, out_ref):
    b, lq, kvl = o_ref.shape
    out_ref[...] = _bdot(o_ref[...].reshape(b * lq, kvl), w_ref[...]).reshape(b, lq, -1).astype(out_ref.dtype)


def _absorb_v(o_lat, w_uv):
    b, nh, lq, kvl = o_lat.shape
    vd = w_uv.shape[-1]
    return pl.pallas_call(
        _absorb_v_kernel,
        grid=(nh,),
        in_specs=[pl.BlockSpec((b, None, lq, kvl), lambda h: (0, h, 0, 0)),
                  pl.BlockSpec((None, kvl, vd), lambda h: (h, 0, 0))],
        out_specs=pl.BlockSpec((b, lq, vd), lambda h: (0, 0, h)),
        out_shape=jax.ShapeDtypeStruct((b, lq, nh * vd), BF16),
        compiler_params=_cparams(("parallel",)),
        name="absorb_v",
    )(o_lat, w_uv)


def _router_kernel(x_ref, w_ref, o_ref):
    o_ref[...] = _bdot(x_ref[...].astype(BF16), w_ref[...])


def _router_logits(x, w):
    t, d = x.shape
    e = w.shape[1]
    tm = math.gcd(t, 1024)
    return pl.pallas_call(
        _router_kernel,
        grid=(t // tm,),
        in_specs=[pl.BlockSpec((tm, d), lambda i: (i, 0)), pl.BlockSpec((d, e), lambda i: (0, 0))],
        out_specs=pl.BlockSpec((tm, e), lambda i: (i, 0)),
        out_shape=jax.ShapeDtypeStruct((t, e), F32),
        compiler_params=_cparams(("parallel",)),
        name="router_logits",
    )(x, w)


def _expert_kernel(be_ref, nv_ref, nb_ref, slot_ref, x_hbm, wg_ref, wu_ref, wd_ref, y_hbm,
                   xbuf, obuf, wgb_ref, wub_ref, wdb_ref, gsem, ssem, *, bm):
    i = pl.program_id(0)
    n_used = nb_ref[0]
    cur = lax.rem(i, 2)
    n_tok = x_hbm.shape[0]

    def start_gather(blk, buf):
        def body(r, carry):
            tok = lax.div(slot_ref[blk * bm + r], TOP_K)
            pltpu.make_async_copy(x_hbm.at[pl.ds(tok, 1), :], xbuf.at[buf, pl.ds(r, 1), :], gsem.at[buf]).start()
            return carry
        lax.fori_loop(0, bm, body, 0, unroll=8)

    def start_scatter(blk, buf):
        def body(r, carry):
            slot = slot_ref[blk * bm + r]
            tok = lax.div(slot, TOP_K)
            dst = (slot - tok * TOP_K) * n_tok + tok
            pltpu.make_async_copy(obuf.at[buf, pl.ds(r, 1), :], y_hbm.at[pl.ds(dst, 1), :], ssem.at[buf]).start()
            return carry
        lax.fori_loop(0, nv_ref[blk], body, 0)

    def wait_scatter(blk, buf):
        n = nv_ref[blk]
        piece = bm
        while piece >= 1:
            @pl.when((n & piece) != 0)
            def _(piece=piece):
                rows, times = (piece, 1) if piece >= SUBLANES else (1, piece)
                for _ in range(times):
                    pltpu.make_async_copy(obuf.at[buf, pl.ds(0, rows), :], y_hbm.at[pl.ds(0, rows), :],
                                          ssem.at[buf]).wait()
            piece //= 2

    @pl.when(i == 0)
    def _():
        start_gather(0, 0)

    @pl.when(i + 1 < n_used)
    def _():
        start_gather(i + 1, 1 - cur)

    @pl.when(i < n_used)
    def _():
        pltpu.make_async_copy(x_hbm.at[pl.ds(0, bm), :], xbuf.at[cur], gsem.at[cur]).wait()

        @pl.when(jnp.logical_or(i == 0, be_ref[i] != be_ref[jnp.maximum(i - 1, 0)]))
        def _():
            wgb_ref[...] = wg_ref[...].astype(BF16)
            wub_ref[...] = wu_ref[...].astype(BF16)
            wdb_ref[...] = wd_ref[...].astype(BF16)

        @pl.when(i >= 2)
        def _():
            wait_scatter(i - 2, cur)

        x = xbuf[cur].astype(BF16)
        g = _bdot(x, wgb_ref[...])
        h = (g * jax.nn.sigmoid(g)) * _bdot(x, wub_ref[...])
        obuf[cur] = _bdot(h.astype(BF16), wdb_ref[...])
        start_scatter(i, cur)

    @pl.when(i == pl.num_programs(0) - 1)
    def _():
        @pl.when(n_used >= 2)
        def _():
            wait_scatter(n_used - 2, lax.rem(n_used, 2))
        wait_scatter(n_used - 1, lax.rem(n_used - 1, 2))


def _expert_mlp(block_e, n_valid, n_used, row_slot, x, w_gate, w_up, w_down, layer):
    t, d = x.shape
    ff = w_gate.shape[-1]
    bm = MOE_ROWS
    n_blocks = block_e.shape[0]
    grid_spec = pltpu.PrefetchScalarGridSpec(
        num_scalar_prefetch=4,
        grid=(n_blocks,),
        in_specs=[pl.BlockSpec(memory_space=pl.ANY),
                  pl.BlockSpec((None, None, d, ff), lambda i, be, nv, nb, sl: (layer, be[i], 0, 0)),
                  pl.BlockSpec((None, None, d, ff), lambda i, be, nv, nb, sl: (layer, be[i], 0, 0)),
                  pl.BlockSpec((None, None, ff, d), lambda i, be, nv, nb, sl: (layer, be[i], 0, 0))],
        out_specs=pl.BlockSpec(memory_space=pl.ANY),
        scratch_shapes=[pltpu.VMEM((2, bm, d), F32), pltpu.VMEM((2, bm, d), F32),
                        pltpu.VMEM((d, ff), BF16), pltpu.VMEM((d, ff), BF16), pltpu.VMEM((ff, d), BF16),
                        pltpu.SemaphoreType.DMA((2,)), pltpu.SemaphoreType.DMA((2,))],
    )
    return pl.pallas_call(
        functools.partial(_expert_kernel, bm=bm),
        grid_spec=grid_spec,
        out_shape=jax.ShapeDtypeStruct((TOP_K * t, d), F32),
        compiler_params=_cparams(("arbitrary",)),
        name="expert_mlp",
    )(block_e, n_valid, n_used, row_slot, x, w_gate, w_up, w_down)


def _first_argmax(x):
    return jnp.argmax(x, axis=-1).astype(jnp.int32)


def _moe(h, w_router, router_bias, w_gate, w_up, w_down, layer):
    t, d = h.shape
    ne = w_router.shape[1]
    ng = N_EXPERT_GROUPS
    eg = ne // ng
    scores = jax.nn.sigmoid(_router_logits(h, w_router))
    sel = scores + router_bias
    grp = sel.reshape(t, ng, eg)
    lane = jnp.arange(eg, dtype=jnp.int32)
    i1 = _first_argmax(grp)
    m1 = jnp.max(grp, axis=-1)
    m2 = jnp.max(jnp.where(lane == i1[..., None], -jnp.inf, grp), axis=-1)
    g_idx = _first_argmax(m1 + m2)
    sel_g = jnp.take_along_axis(grp, g_idx[:, None, None], axis=1)[:, 0]
    sc_g = jnp.take_along_axis(scores.reshape(t, ng, eg), g_idx[:, None, None], axis=1)[:, 0]
    e1 = _first_argmax(sel_g)
    e2 = _first_argmax(jnp.where(lane == e1[:, None], -jnp.inf, sel_g))
    w1 = jnp.take_along_axis(sc_g, e1[:, None], axis=1)[:, 0]
    w2 = jnp.take_along_axis(sc_g, e2[:, None], axis=1)[:, 0]
    wsum = w1 + w2
    e_idx = jnp.stack([g_idx * eg + e1, g_idx * eg + e2], axis=1)
    w = jnp.stack([w1 / wsum, w2 / wsum], axis=1)

    bm = MOE_ROWS
    n_assign = t * TOP_K
    flat_e = e_idx.reshape(-1)
    order = jnp.argsort(flat_e).astype(jnp.int32)
    counts = jnp.sum((flat_e[:, None] == jnp.arange(ne, dtype=jnp.int32)[None, :]).astype(jnp.int32), axis=0)
    starts = jnp.cumsum(counts) - counts
    pcounts = (counts + bm - 1) // bm * bm
    pends = jnp.cumsum(pcounts)
    pstarts = pends - pcounts
    n_blocks = -(-n_assign // bm) + ne
    blk_start = jnp.arange(n_blocks, dtype=jnp.int32) * bm
    block_e = jnp.minimum(jnp.sum((pends[None, :] <= blk_start[:, None]).astype(jnp.int32), axis=1), ne - 1)
    n_used = (pends[ne - 1:] // bm).astype(jnp.int32)
    blk_off = blk_start - pstarts[block_e]
    n_valid = jnp.clip(counts[block_e] - blk_off, 0, bm).astype(jnp.int32)
    off = blk_off[:, None] + jnp.arange(bm, dtype=jnp.int32)[None, :]
    src = jnp.clip(starts[block_e][:, None] + off, 0, n_assign - 1)
    row_slot = jnp.where(off < counts[block_e][:, None], order[src.reshape(-1)].reshape(n_blocks, bm), 0)
    y = _expert_mlp(block_e, n_valid, n_used, row_slot.reshape(-1), h, w_gate, w_up, w_down, layer)
    return y, w


def _rope_tables(pos, rope):
    inv = 1.0 / (ROPE_THETA ** (jnp.arange(0, rope, 2, dtype=F32) / rope))
    ang = pos.astype(F32)[:, None] * inv[None, :]
    cos, sin = jnp.cos(ang), jnp.sin(ang)
    pad = jnp.zeros((pos.shape[0], LANES - rope), F32)
    return jnp.concatenate([cos, cos, pad], axis=1), jnp.concatenate([-sin, sin, pad], axis=1)


def kernel(x_prompt, x_sample, c_prompt, c_sample, state_ssm_re, state_ssm_im, cache_ckv, cache_krope, w_ada, b_ada, ln_g, ln_b, ssm_a_re, ssm_a_im, ssm_log_dt, ssm_b_re, ssm_b_im, ssm_c_re, ssm_c_im, ssm_d, ssm_w_glu, mla_w_in, mla_q_norm, mla_kv_norm, mla_w_uq, mla_w_uk, mla_w_uv, mla_w_o, w_router, router_bias, moe_w_gate, moe_w_up, moe_w_down):
    depth, d = w_ada.shape[0], w_ada.shape[1]
    alpha = (2 * depth) ** 0.25
    nh, nope, vd = mla_w_uk.shape[2], mla_w_uk.shape[3], mla_w_uv.shape[3]
    ql, kvl = mla_q_norm.shape[1], mla_kv_norm.shape[1]
    rope = mla_w_uq.shape[3] - nope
    assert nope == LANES and vd == LANES and rope <= LANES
    scale = (nope + rope) ** -0.5 * math.log2(math.e)
    n_groups, n_state = ssm_a_re.shape[1], ssm_a_re.shape[2]
    past = cache_ckv.shape[2]

    xs = [x_prompt, x_sample]
    bsz = [x.shape[0] for x in xs]
    seq = [x.shape[1] for x in xs]
    row0 = [0, bsz[0] * seq[0]]
    n_tokens = row0[1] + bsz[1] * seq[1]
    pos = [jnp.arange(seq[0]), past + jnp.arange(seq[1])]
    tabs = [_rope_tables(p, rope) for p in pos]

    mod = _ada_mod(jnp.concatenate([c_prompt, c_sample], axis=0), w_ada, b_ada)

    def mod_rows(i, grp, piece):
        b0 = 0 if grp == 0 else bsz[0]
        return mod[i, b0:b0 + bsz[grp], piece * d:(piece + 1) * d].reshape(bsz[grp], 1, d)

    assert depth % 2 == 0, "layers alternate S5 / MLA mixers, starting with S5"
    w_router_b = w_router.astype(BF16)
    half = rope // 2
    ssm_out = [[[], []], [[], []]]
    ckv_out, kr_out = [[], []], [[], []]

    h_in = [None, None]
    for i in range(depth):
        j = i // 2
        x1s, tokens = [], jnp.zeros((n_tokens, d), F32)
        if i % 2 == 0:
            bd, cd, ab = _s5_tables(ssm_a_re[j], ssm_a_im[j], ssm_log_dt[j], ssm_b_re[j], ssm_b_im[j],
                                    ssm_c_re[j], ssm_c_im[j])
            w_glu = ssm_w_glu[j].astype(BF16)
        else:
            w_in = mla_w_in[j]
            r0 = ql + kvl
            zc = jnp.zeros((d, LANES - rope), F32)
            w_in_ext = jnp.concatenate([w_in, zc, w_in[:, r0 + half:], w_in[:, r0:r0 + half], zc], axis=1).astype(BF16)
            wq = mla_w_uq[j]
            zq = jnp.zeros((ql, nh, LANES - rope), F32)
            wq_rope = jnp.concatenate([wq[..., nope:], zq], axis=-1)
            wq_rot = jnp.concatenate([wq[..., nope + half:], wq[..., nope:nope + half], zq], axis=-1)
            wq = jnp.concatenate([wq[..., :nope].reshape(ql, -1), wq_rope.reshape(ql, -1),
                                  wq_rot.reshape(ql, -1)], axis=1).astype(BF16)
            wkv = jnp.concatenate([mla_w_uk[j].reshape(kvl, -1), mla_w_uv[j].reshape(kvl, -1)],
                                  axis=1).astype(BF16)
            w_ukt = mla_w_uk[j].transpose(1, 2, 0).astype(BF16)
            w_uvh = mla_w_uv[j].transpose(1, 0, 2).astype(BF16)
            w_o = mla_w_o[j].astype(BF16)
        for grp in range(2):
            x = xs[grp]
            b, l = bsz[grp], seq[grp]
            sh1, sc1, g1, sh2, sc2 = (mod_rows(i, grp, p) for p in range(5))
            lng, lnb = ln_g[i, 0].reshape(1, d), ln_b[i, 0].reshape(1, d)
            if i % 2 == 0:
                u = _modulate_to_s5_layout(x, sc1, sh1)
                nbg = b // SUBLANES
                if grp == 0:
                    h0r = jnp.zeros((nbg, SUBLANES, n_groups * n_state), F32)
                    h0i = h0r
                else:
                    h0r = state_ssm_re[j].reshape(nbg, SUBLANES, n_groups * n_state)
                    h0i = state_ssm_im[j].reshape(nbg, SUBLANES, n_groups * n_state)
                z, hr, hi = _s5_scan(u, bd, cd, ab, ssm_d[j].reshape(1, d), h0r, h0i)
                ssm_out[grp][0].append(hr.reshape(b, n_groups, n_state))
                ssm_out[grp][1].append(hi.reshape(b, n_groups, n_state))
                z = z.reshape(nbg, l, SUBLANES * d)

                def a_spec(tm):
                    return pl.BlockSpec((None, tm, d), lambda bi, t: (bi // SUBLANES, t, bi % SUBLANES))

                x1, tokens = _mix_ln(z, a_spec, w_glu, x, g1, lng, lnb, sc2, sh2, tokens, row0[grp],
                                     glu=True, alpha=alpha)
            else:
                cos_t, sin_t = tabs[grp]
                cq, ckv, kr, krp = _mla_in(h_in[grp], w_in_ext, mla_q_norm[j].reshape(1, ql),
                                           mla_kv_norm[j].reshape(1, kvl), cos_t, sin_t, ql=ql, kvl=kvl, rope=rope)
                q = _mla_q(cq, wq, cos_t, sin_t, nh=nh, scale=scale)
                if grp == 0:
                    k, v = _mla_kv(ckv, krp, wkv, nh=nh)
                    o = _prompt_attn(q, k, v, vd=vd)
                else:
                    q_lat = _absorb_q(q, w_ukt, nope=nope)
                    o_lat = _cached_attn(q_lat, q, cache_ckv[j], cache_krope[j], ckv, kr, nope=nope, rope=rope)
                    o = _absorb_v(o_lat, w_uvh)
                ckv_out[grp].append(ckv)
                kr_out[grp].append(kr)

                def a_spec(tm):
                    return pl.BlockSpec((None, tm, nh * vd), lambda bi, t: (bi, t, 0))

                x1, tokens = _mix_ln(o, a_spec, w_o, x, g1, lng, lnb, sc2, sh2, tokens, row0[grp],
                                     glu=False, alpha=alpha)
            x1s.append(x1)
        y_all, wt_all = _moe(tokens, w_router_b, router_bias, moe_w_gate, moe_w_up, moe_w_down, i)
        for grp in range(2):
            g2 = mod_rows(i, grp, 5)
            next_mod = (mod_rows(i + 1, grp, 1), mod_rows(i + 1, grp, 0)) if i % 2 == 0 else None
            xs[grp], h_in[grp] = _ffn_ln(y_all, wt_all, row0[grp], x1s[grp], g2, ln_g[i, 1].reshape(1, d),
                                         ln_b[i, 1].reshape(1, d), next_mod, alpha=alpha)

    return (xs[0], xs[1],
            jnp.stack(ssm_out[0][0]), jnp.stack(ssm_out[0][1]), jnp.stack(ckv_out[0]), jnp.stack(kr_out[0]),
            jnp.stack(ssm_out[1][0]), jnp.stack(ssm_out[1][1]), jnp.stack(ckv_out[1]), jnp.stack(kr_out[1]))
```

```python
import functools
import math

import jax
import jax.numpy as jnp
from jax import lax
from jax.experimental import pallas as pl
from jax.experimental.pallas import tpu as pltpu

F32 = jnp.float32
BF16 = jnp.bfloat16

CHUNK = 64
SSM_GROUP_WIDTH = 16
N_EXPERT_GROUPS = 8
TOP_K = 2
ROPE_THETA = 10000.0
LN_EPS = 1e-5
RMS_EPS = 1e-6
NEG_INF = -1e30

V7X_VMEM_BYTES = 64 * 1024 * 1024
SUBLANES = 8
LANES = 128
MXU_DIM = 256

VMEM_LIMIT = 56 * 1024 * 1024
S5_CHANNELS = MXU_DIM
S5_TIME = 128
ROW_TILE = 256
ATTN_BLOCK = 512
ATTN_HEADS = 4
MOE_ROWS = 256
CACHE_KEYS = 1024


def _cparams(sem):
    return pltpu.CompilerParams(dimension_semantics=sem, vmem_limit_bytes=VMEM_LIMIT)


def _bdot(a, b):
    return jnp.dot(a, b, preferred_element_type=F32)


def _ada_kernel(c_ref, w_ref, b_ref, o_ref):
    c = c_ref[...]
    cs = (c * jax.nn.sigmoid(c)).astype(BF16)
    o_ref[...] = _bdot(cs, w_ref[...].astype(BF16)) + b_ref[...]


def _ada_mod(c_all, w_ada, b_ada):
    depth, d, n = w_ada.shape
    bc = c_all.shape[0]
    tn = 1024
    return pl.pallas_call(
        _ada_kernel,
        grid=(depth, n // tn),
        in_specs=[pl.BlockSpec((bc, d), lambda i, j: (0, 0)),
                  pl.BlockSpec((None, d, tn), lambda i, j: (i, 0, j)),
                  pl.BlockSpec((None, 1, tn), lambda i, j: (i, 0, j))],
        out_specs=pl.BlockSpec((None, bc, tn), lambda i, j: (i, 0, j)),
        out_shape=jax.ShapeDtypeStruct((depth, bc, n), F32),
        compiler_params=_cparams(("parallel", "parallel")),
        name="ada_mod",
    )(c_all, w_ada, b_ada.reshape(depth, 1, n))


def _modulate_kernel(x_ref, sc_ref, sh_ref, o_ref):
    o_ref[...] = x_ref[...] * (1.0 + sc_ref[...]) + sh_ref[...]


def _modulate_to_s5_layout(x, sc, sh):
    b, l, d = x.shape
    tl = min(l, 512)
    out = pl.pallas_call(
        _modulate_kernel,
        grid=(b, l // tl),
        in_specs=[pl.BlockSpec((None, tl, d), lambda i, t: (i, t, 0)),
                  pl.BlockSpec((None, 1, d), lambda i, t: (i, 0, 0)),
                  pl.BlockSpec((None, 1, d), lambda i, t: (i, 0, 0))],
        out_specs=pl.BlockSpec((None, tl, d), lambda i, t: (i // SUBLANES, t, i % SUBLANES)),
        out_shape=jax.ShapeDtypeStruct((b // SUBLANES, l, SUBLANES * d), F32),
        compiler_params=_cparams(("parallel", "parallel")),
        name="modulate_s5",
    )(x, sc, sh)
    return out.reshape(b // SUBLANES, l, SUBLANES, d)


def _s5_kernel(u_ref, bd_ref, cd_ref, ab_ref, d_ref, h0r_ref, h0i_ref,
               z_ref, hr_ref, hi_ref, xs_ref, hs_ref, h_ref, *, tt, sb):
    t = pl.program_id(2)

    @pl.when(t == 0)
    def _():
        h_ref[0] = h0r_ref[...]
        h_ref[1] = h0i_ref[...]

    cb = u_ref.shape[-1]
    u = u_ref[...].reshape(tt * SUBLANES, cb)
    xs_ref[...] = _bdot(u.astype(BF16), bd_ref[...])
    a_re = jnp.broadcast_to(ab_ref[0:1, :], (SUBLANES, sb))
    a_im = jnp.broadcast_to(ab_ref[1:2, :], (SUBLANES, sb))

    pair = 2 * SUBLANES

    def step(i, carry):
        h_re, h_im = carry
        r0 = pl.multiple_of(i * pair, pair)
        res, ims = [], []
        for s in range(2):
            x_re = xs_ref[pl.ds(r0 + s * SUBLANES, SUBLANES), 0:sb]
            x_im = xs_ref[pl.ds(r0 + s * SUBLANES, SUBLANES), sb:2 * sb]
            h_re, h_im = a_re * h_re - a_im * h_im + x_re, a_re * h_im + a_im * h_re + x_im
            res.append(h_re)
            ims.append(h_im)
        hs_ref[pl.ds(r0, pair), 0:sb] = jnp.concatenate(res, axis=0).astype(BF16)
        hs_ref[pl.ds(r0, pair), sb:2 * sb] = jnp.concatenate(ims, axis=0).astype(BF16)
        return h_re, h_im

    h_re, h_im = lax.fori_loop(0, tt // 2, step, (h_ref[0], h_ref[1]))
    h_ref[0] = h_re
    h_ref[1] = h_im
    y = _bdot(hs_ref[...], cd_ref[...]) + d_ref[...] * u
    z_ref[...] = jax.nn.gelu(y).astype(z_ref.dtype)

    @pl.when(t == pl.num_programs(2) - 1)
    def _():
        hr_ref[...] = h_re
        hi_ref[...] = h_im


def _s5_scan(u, bd, cd, ab, d_skip, h0_re, h0_im):
    nbg, l, _, d = u.shape
    nj, cb, sb2 = bd.shape
    sb = sb2 // 2
    tt = min(l, S5_TIME)
    nt = l // tt
    kern = functools.partial(_s5_kernel, tt=tt, sb=sb)
    state_spec = pl.BlockSpec((None, SUBLANES, sb), lambda g, j, t: (g, 0, j))
    state_shape = jax.ShapeDtypeStruct((nbg, SUBLANES, nj * sb), F32)
    return pl.pallas_call(
        kern,
        grid=(nbg, nj, nt),
        in_specs=[pl.BlockSpec((None, tt, SUBLANES, cb), lambda g, j, t: (g, t, 0, j)),
                  pl.BlockSpec((None, cb, sb2), lambda g, j, t: (j, 0, 0)),
                  pl.BlockSpec((None, sb2, cb), lambda g, j, t: (j, 0, 0)),
                  pl.BlockSpec((None, 2, sb), lambda g, j, t: (j, 0, 0)),
                  pl.BlockSpec((1, cb), lambda g, j, t: (0, j)),
                  state_spec, state_spec],
        out_specs=[pl.BlockSpec((tt * SUBLANES, cb), lambda g, j, t: (g * nt + t, j)),
                   state_spec, state_spec],
        out_shape=[jax.ShapeDtypeStruct((nbg * l * SUBLANES, d), BF16), state_shape, state_shape],
        scratch_shapes=[pltpu.VMEM((tt * SUBLANES, sb2), F32), pltpu.VMEM((tt * SUBLANES, sb2), BF16),
                        pltpu.VMEM((2, SUBLANES, sb), F32)],
        compiler_params=_cparams(("parallel", "parallel", "arbitrary")),
        name="s5_scan",
    )(u, bd, cd, ab, d_skip, h0_re, h0_im)


def _s5_tables(a_re, a_im, log_dt, b_re, b_im, c_re, c_im):
    g, p = a_re.shape
    w = b_re.shape[-1]
    gl = S5_CHANNELS // w
    nj = g // gl
    dt = jnp.exp(log_dt)[:, None]
    mag = jnp.exp(a_re * dt)
    abar_re, abar_im = mag * jnp.cos(a_im * dt), mag * jnp.sin(a_im * dt)
    den = a_re * a_re + a_im * a_im
    n_re, n_im = abar_re - 1.0, abar_im
    f_re = (n_re * a_re + n_im * a_im) / den
    f_im = (n_im * a_re - n_re * a_im) / den
    bb_re = f_re[..., None] * b_re - f_im[..., None] * b_im
    bb_im = f_re[..., None] * b_im + f_im[..., None] * b_re
    eye = jnp.eye(gl, dtype=F32)
    bb = jnp.stack([bb_re, bb_im]).reshape(2, nj, gl, p, w)
    bd = jnp.einsum('sjgpw,gh->jgwshp', bb, eye).reshape(nj, gl * w, 2 * gl * p)
    cc = jnp.stack([c_re, -c_im]).reshape(2, nj, gl, w, p)
    cd = jnp.einsum('sjgwp,gh->jsgphw', cc, eye).reshape(nj, 2 * gl * p, gl * w)
    ab = jnp.stack([abar_re.reshape(nj, gl * p), abar_im.reshape(nj, gl * p)], axis=1)
    return bd.astype(BF16), cd.astype(BF16), ab


def _layer_norm(y, g, b):
    mu = jnp.mean(y, axis=-1, keepdims=True)
    yc = y - mu
    var = jnp.mean(yc * yc, axis=-1, keepdims=True)
    return yc * lax.rsqrt(var + LN_EPS) * g + b


def _mix_ln_kernel(a_ref, w_ref, x_ref, gate_ref, lng_ref, lnb_ref, sc_ref, sh_ref, tokens_hbm,
                   x1_ref, h_ref, *, glu, alpha):
    del tokens_hbm
    acc = _bdot(a_ref[...], w_ref[...])
    if glu:
        d = acc.shape[-1] // 2
        m = acc[:, :d] * jax.nn.sigmoid(acc[:, d:])
    else:
        m = acc
    x1 = _layer_norm(alpha * x_ref[...] + gate_ref[...] * m, lng_ref[...], lnb_ref[...])
    x1_ref[...] = x1
    h_ref[...] = x1 * (1.0 + sc_ref[...]) + sh_ref[...]


def _mix_ln(a, a_spec, w, x, gate, ln_g, ln_b, sc, sh, tokens, row0, *, glu, alpha):
    b, l, d = x.shape
    tm = min(l, ROW_TILE)
    nt = l // tm
    blk0 = row0 // tm
    k, n = w.shape
    row = pl.BlockSpec((None, 1, d), lambda i, t: (i, 0, 0))
    vec = pl.BlockSpec((1, d), lambda i, t: (0, 0))
    tile = pl.BlockSpec((None, tm, d), lambda i, t: (i, t, 0))
    return pl.pallas_call(
        functools.partial(_mix_ln_kernel, glu=glu, alpha=alpha),
        grid=(b, nt),
        in_specs=[a_spec(tm), pl.BlockSpec((k, n), lambda i, t: (0, 0), pipeline_mode=pl.Buffered(1)),
                  tile, row, vec, vec, row, row, pl.BlockSpec(memory_space=pl.ANY)],
        out_specs=[tile, pl.BlockSpec((tm, d), lambda i, t: (blk0 + i * nt + t, 0))],
        out_shape=[jax.ShapeDtypeStruct((b, l, d), F32), jax.ShapeDtypeStruct(tokens.shape, F32)],
        input_output_aliases={8: 1},
        compiler_params=_cparams(("parallel", "parallel")),
        name="mix_ln_glu" if glu else "mix_ln_proj",
    )(a, w, x, gate, ln_g, ln_b, sc, sh, tokens)


def _ffn_ln_kernel(*refs, alpha):
    y_refs, (wt_ref, x_ref, gate_ref, lng_ref, lnb_ref), rest = refs[:TOP_K], refs[TOP_K:TOP_K + 5], refs[TOP_K + 5:]
    f = wt_ref[:, 0:1] * y_refs[0][...]
    for k in range(1, TOP_K):
        f = f + wt_ref[:, k:k + 1] * y_refs[k][...]
    x2 = _layer_norm(alpha * x_ref[...] + gate_ref[...] * f, lng_ref[...], lnb_ref[...])
    if len(rest) == 1:
        rest[0][...] = x2
    else:
        sc_ref, sh_ref, x2_ref, h_ref = rest
        x2_ref[...] = x2
        h_ref[...] = (x2 * (1.0 + sc_ref[...]) + sh_ref[...]).astype(h_ref.dtype)


def _ffn_ln(y_all, wt_all, row0, x, gate, ln_g, ln_b, next_mod, *, alpha):
    b, l, d = x.shape
    tm = min(l, ROW_TILE)
    nt = l // tm
    blk0 = row0 // tm
    kblk = wt_all.shape[0] // tm
    row = pl.BlockSpec((None, 1, d), lambda i, t: (i, 0, 0))
    vec = pl.BlockSpec((1, d), lambda i, t: (0, 0))
    tile = pl.BlockSpec((None, tm, d), lambda i, t: (i, t, 0))
    in_specs = [pl.BlockSpec((tm, d), functools.partial(lambda k, i, t: (k * kblk + blk0 + i * nt + t, 0), k))
                for k in range(TOP_K)]
    in_specs += [pl.BlockSpec((tm, TOP_K), lambda i, t: (blk0 + i * nt + t, 0)), tile, row, vec, vec]
    x2_shape = jax.ShapeDtypeStruct((b, l, d), F32)
    if next_mod is None:
        args, out_specs, out_shape = (), tile, x2_shape
    else:
        args, in_specs = tuple(next_mod), in_specs + [row, row]
        out_specs, out_shape = [tile, tile], [x2_shape, jax.ShapeDtypeStruct((b, l, d), BF16)]
    out = pl.pallas_call(
        functools.partial(_ffn_ln_kernel, alpha=alpha),
        grid=(b, nt),
        in_specs=in_specs,
        out_specs=out_specs,
        out_shape=out_shape,
        compiler_params=_cparams(("parallel", "parallel")),
        name="ffn_ln",
    )(*([y_all] * TOP_K), wt_all, x, gate, ln_g, ln_b, *args)
    return (out, None) if next_mod is None else out


def _rms_norm(x, g):
    ms = jnp.mean(x * x, axis=-1, keepdims=True)
    return x * lax.rsqrt(ms + RMS_EPS) * g


def _mla_in_kernel(h_ref, w_ref, qn_ref, kvn_ref, cos_ref, sin_ref,
                   cq_ref, ckv_ref, kr_ref, krp_ref, *, ql, kvl, rope):
    acc = _bdot(h_ref[...], w_ref[...])
    cq_ref[...] = _rms_norm(acc[:, :ql], qn_ref[...]).astype(cq_ref.dtype)
    ckv_ref[...] = _rms_norm(acc[:, ql:ql + kvl], kvn_ref[...])
    o = ql + kvl
    kr = acc[:, o:o + LANES] * cos_ref[...] + acc[:, o + LANES:o + 2 * LANES] * sin_ref[...]
    kr_ref[...] = kr[:, :rope]
    krp_ref[...] = kr.astype(krp_ref.dtype)


def _mla_in(h, w_in_ext, q_norm, kv_norm, cos_t, sin_t, *, ql, kvl, rope):
    b, l, d = h.shape
    tm = min(l, ROW_TILE)
    n = w_in_ext.shape[1]
    kern = functools.partial(_mla_in_kernel, ql=ql, kvl=kvl, rope=rope)
    tab = pl.BlockSpec((tm, LANES), lambda i, t: (t, 0))

    def out(nf):
        return pl.BlockSpec((None, tm, nf), lambda i, t: (i, t, 0))

    return pl.pallas_call(
        kern,
        grid=(b, l // tm),
        in_specs=[pl.BlockSpec((None, tm, d), lambda i, t: (i, t, 0)),
                  pl.BlockSpec((d, n), lambda i, t: (0, 0)),
                  pl.BlockSpec((1, ql), lambda i, t: (0, 0)),
                  pl.BlockSpec((1, kvl), lambda i, t: (0, 0)), tab, tab],
        out_specs=[out(ql), out(kvl), out(rope), out(LANES)],
        out_shape=[jax.ShapeDtypeStruct((b, l, ql), BF16), jax.ShapeDtypeStruct((b, l, kvl), F32),
                   jax.ShapeDtypeStruct((b, l, rope), F32), jax.ShapeDtypeStruct((b, l, LANES), BF16)],
        compiler_params=_cparams(("parallel", "parallel")),
        name="mla_in",
    )(h, w_in_ext, q_norm, kv_norm, cos_t, sin_t)


def _mla_q_kernel(cq_ref, w_ref, cos_ref, sin_ref, q_ref, *, nh, scale):
    acc = _bdot(cq_ref[...], w_ref[...])
    cos = cos_ref[...] * scale
    sin = sin_ref[...] * scale
    for h in range(nh):
        q_ref[h, :, :LANES] = (acc[:, h * LANES:(h + 1) * LANES] * scale).astype(q_ref.dtype)
        r0, p0 = (nh + h) * LANES, (2 * nh + h) * LANES
        q_ref[h, :, LANES:] = (acc[:, r0:r0 + LANES] * cos + acc[:, p0:p0 + LANES] * sin).astype(q_ref.dtype)


def _mla_q(cq, wq, cos_t, sin_t, *, nh, scale):
    b, l, ql = cq.shape
    n = wq.shape[1]
    tm = min(l, ROW_TILE)
    tab = pl.BlockSpec((tm, LANES), lambda i, t: (t, 0))
    return pl.pallas_call(
        functools.partial(_mla_q_kernel, nh=nh, scale=scale),
        grid=(b, l // tm),
        in_specs=[pl.BlockSpec((None, tm, ql), lambda i, t: (i, t, 0)),
                  pl.BlockSpec((ql, n), lambda i, t: (0, 0)), tab, tab],
        out_specs=pl.BlockSpec((None, nh, tm, 2 * LANES), lambda i, t: (i, 0, t, 0)),
        out_shape=jax.ShapeDtypeStruct((b, nh, l, 2 * LANES), BF16),
        compiler_params=_cparams(("parallel", "parallel")),
        name="mla_q",
    )(cq, wq, cos_t, sin_t)


def _mla_kv_kernel(ckv_ref, krp_ref, w_ref, k_ref, v_ref, *, nh):
    acc = _bdot(ckv_ref[...].astype(BF16), w_ref[...])
    krp = krp_ref[...]
    lane = lax.broadcasted_iota(jnp.int32, krp.shape, 1)
    ones_col = jnp.where(lane == 0, 1.0, 0.0).astype(v_ref.dtype)
    for h in range(nh):
        k_ref[h, :, :LANES] = acc[:, h * LANES:(h + 1) * LANES].astype(k_ref.dtype)
        k_ref[h, :, LANES:] = krp
        v_ref[h, :, :LANES] = acc[:, (nh + h) * LANES:(nh + h + 1) * LANES].astype(v_ref.dtype)
        v_ref[h, :, LANES:] = ones_col


def _mla_kv(ckv, krp, wkv, *, nh):
    b, l, kvl = ckv.shape
    n = wkv.shape[1]
    tm = min(l, ROW_TILE)
    head_tile = pl.BlockSpec((None, nh, tm, 2 * LANES), lambda i, t: (i, 0, t, 0))
    head_shape = jax.ShapeDtypeStruct((b, nh, l, 2 * LANES), BF16)
    return pl.pallas_call(
        functools.partial(_mla_kv_kernel, nh=nh),
        grid=(b, l // tm),
        in_specs=[pl.BlockSpec((None, tm, kvl), lambda i, t: (i, t, 0)),
                  pl.BlockSpec((None, tm, LANES), lambda i, t: (i, t, 0)),
                  pl.BlockSpec((kvl, n), lambda i, t: (0, 0))],
        out_specs=[head_tile, head_tile],
        out_shape=[head_shape, head_shape],
        compiler_params=_cparams(("parallel", "parallel")),
        name="mla_kv",
    )(ckv, krp, wkv)


def _qk(q, k):
    return lax.dot_general(q, k, (((1,), (1,)), ((), ())), preferred_element_type=F32)


def _softmax_step(carry, s, v):
    m, l, acc = carry
    m_new = jnp.maximum(m, jnp.max(s, axis=-1, keepdims=True))
    p = jnp.exp2(s - m_new)
    alpha = jnp.exp2(m - m_new)
    l = alpha * l + jnp.sum(p, axis=-1, keepdims=True)
    acc = alpha * acc + _bdot(p.astype(BF16), v)
    return m_new, l, acc


def _softmax_step_ones(carry, s, v):
    m, acc = carry
    m_new = jnp.maximum(m, jnp.max(s, axis=-1, keepdims=True))
    p = jnp.exp2(s - m_new)
    acc = jnp.exp2(m - m_new) * acc + _bdot(p.astype(BF16), v)
    return m_new, acc


def _prompt_attn_kernel(q_ref, k_ref, v_ref, o_ref, *, blk, vd):
    i = pl.program_id(2)
    heads = range(q_ref.shape[0])
    qs = [q_ref[h] for h in heads]

    def body(j, carry):
        r0 = pl.multiple_of(j * blk, blk)
        return tuple(_softmax_step_ones(carry[h], _qk(qs[h], k_ref[h, pl.ds(r0, blk), :]),
                                        v_ref[h, pl.ds(r0, blk), :]) for h in heads)

    init = (jnp.full((blk, 1), NEG_INF, F32), jnp.zeros((blk, v_ref.shape[-1]), F32))
    carry = lax.fori_loop(0, i, body, tuple(init for _ in heads))
    r0 = pl.multiple_of(i * blk, blk)
    q_chunk = lax.broadcasted_iota(jnp.int32, (blk, blk), 0) // CHUNK
    k_chunk = lax.broadcasted_iota(jnp.int32, (blk, blk), 1) // CHUNK
    visible = k_chunk <= q_chunk
    for h in heads:
        s = jnp.where(visible, _qk(qs[h], k_ref[h, pl.ds(r0, blk), :]), NEG_INF)
        _, acc = _softmax_step_ones(carry[h], s, v_ref[h, pl.ds(r0, blk), :])
        o_ref[:, h * vd:(h + 1) * vd] = (acc[:, :vd] / acc[:, vd:vd + 1]).astype(o_ref.dtype)


def _prompt_attn(q, k, v, *, vd):
    b, nh, l, dk = q.shape
    blk = min(l, ATTN_BLOCK)
    hp = ATTN_HEADS
    return pl.pallas_call(
        functools.partial(_prompt_attn_kernel, blk=blk, vd=vd),
        grid=(b, nh // hp, l // blk),
        in_specs=[pl.BlockSpec((None, hp, blk, dk), lambda i, h, t: (i, h, t, 0)),
                  pl.BlockSpec((None, hp, l, dk), lambda i, h, t: (i, h, 0, 0)),
                  pl.BlockSpec((None, hp, l, v.shape[-1]), lambda i, h, t: (i, h, 0, 0))],
        out_specs=pl.BlockSpec((None, blk, hp * vd), lambda i, h, t: (i, t, h)),
        out_shape=jax.ShapeDtypeStruct((b, l, nh * vd), BF16),
        compiler_params=_cparams(("parallel", "parallel", "arbitrary")),
        name="prompt_attn",
    )(q, k, v)


def _absorb_q_kernel(q_ref, w_ref, o_ref, *, nope):
    b, lq, dk = q_ref.shape
    q = q_ref[...].reshape(b * lq, dk)[:, :nope]
    o_ref[...] = _bdot(q, w_ref[...]).reshape(b, lq, -1).astype(o_ref.dtype)


def _absorb_q(q, w_ukt, *, nope):
    b, nh, lq, dk = q.shape
    kvl = w_ukt.shape[-1]
    return pl.pallas_call(
        functools.partial(_absorb_q_kernel, nope=nope),
        grid=(nh,),
        in_specs=[pl.BlockSpec((b, None, lq, dk), lambda h: (0, h, 0, 0)),
                  pl.BlockSpec((None, nope, kvl), lambda h: (h, 0, 0))],
        out_specs=pl.BlockSpec((b, None, lq, kvl), lambda h: (0, h, 0, 0)),
        out_shape=jax.ShapeDtypeStruct((b, nh, lq, kvl), BF16),
        compiler_params=_cparams(("parallel",)),
        name="absorb_q",
    )(q, w_ukt)


def _cached_attn_kernel(ql_ref, q_ref, cc_ref, ck_ref, nc_ref, nk_ref, o_ref, *, nope, rope, past, kb):
    nh, lq, kvl = ql_ref.shape
    rows = nh * lq
    ql = ql_ref[...].reshape(rows, kvl)
    qr = q_ref[...].reshape(rows, -1)[:, nope:nope + rope]
    q_pos = past + lax.broadcasted_iota(jnp.int32, (rows, 1), 0) % lq
    carry = (jnp.full((rows, 1), NEG_INF, F32), jnp.zeros((rows, 1), F32), jnp.zeros((rows, kvl), F32))

    def scores(c, kr, k0):
        s = _qk(ql, c) + _qk(qr, kr)
        k_pos = k0 + lax.broadcasted_iota(jnp.int32, s.shape, 1)
        return jnp.where(k_pos // CHUNK <= q_pos // CHUNK, s, NEG_INF)

    for j in range(past // kb):
        c = cc_ref[j * kb:(j + 1) * kb, :].astype(BF16)
        kr = ck_ref[j * kb:(j + 1) * kb, :].astype(BF16)
        carry = _softmax_step(carry, scores(c, kr, j * kb), c)
    c = nc_ref[...].astype(BF16)
    kr = nk_ref[...].astype(BF16)
    _, l, acc = _softmax_step(carry, scores(c, kr, past), c)
    o_ref[...] = (acc / l).reshape(nh, lq, kvl).astype(o_ref.dtype)


def _cached_attn(q_lat, q, cache_ckv, cache_kr, new_ckv, new_kr, *, nope, rope):
    b, nh, lq, kvl = q_lat.shape
    past = cache_ckv.shape[1]
    kb = min(past, CACHE_KEYS)
    dk = q.shape[-1]
    kern = functools.partial(_cached_attn_kernel, nope=nope, rope=rope, past=past, kb=kb)
    return pl.pallas_call(
        kern,
        grid=(b,),
        in_specs=[pl.BlockSpec((None, nh, lq, kvl), lambda i: (i, 0, 0, 0)),
                  pl.BlockSpec((None, nh, lq, dk), lambda i: (i, 0, 0, 0)),
                  pl.BlockSpec((None, past, kvl), lambda i: (i, 0, 0)),
                  pl.BlockSpec((None, past, rope), lambda i: (i, 0, 0)),
                  pl.BlockSpec((None, lq, kvl), lambda i: (i, 0, 0)),
                  pl.BlockSpec((None, lq, rope), lambda i: (i, 0, 0))],
        out_specs=pl.BlockSpec((None, nh, lq, kvl), lambda i: (i, 0, 0, 0)),
        out_shape=jax.ShapeDtypeStruct((b, nh, lq, kvl), BF16),
        compiler_params=_cparams(("parallel",)),
        name="cached_attn",
    )(q_lat, q, cache_ckv, cache_kr, new_ckv, new_kr)


def _absorb_v_kernel(o_ref, w_ref, out_ref):
    b, lq, kvl = o_ref.shape
    out_ref[...] = _bdot(o_ref[...].reshape(b * lq, kvl), w_ref[...]).reshape(b, lq, -1).astype(out_ref.dtype)


def _absorb_v(o_lat, w_uv):
    b, nh, lq, kvl = o_lat.shape
    vd = w_uv.shape[-1]
    return pl.pallas_call(
        _absorb_v_kernel,
        grid=(nh,),
        in_specs=[pl.BlockSpec((b, None, lq, kvl), lambda h: (0, h, 0, 0)),
                  pl.BlockSpec((None, kvl, vd), lambda h: (h, 0, 0))],
        out_specs=pl.BlockSpec((b, lq, vd), lambda h: (0, 0, h)),
        out_shape=jax.ShapeDtypeStruct((b, lq, nh * vd), BF16),
        compiler_params=_cparams(("parallel",)),
        name="absorb_v",
    )(o_lat, w_uv)


def _router_kernel(x_ref, w_ref, o_ref):
    o_ref[...] = _bdot(x_ref[...].astype(BF16), w_ref[...])


def _router_logits(x, w):
    t, d = x.shape
    e = w.shape[1]
    tm = math.gcd(t, 1024)
    return pl.pallas_call(
        _router_kernel,
        grid=(t // tm,),
        in_specs=[pl.BlockSpec((tm, d), lambda i: (i, 0)), pl.BlockSpec((d, e), lambda i: (0, 0))],
        out_specs=pl.BlockSpec((tm, e), lambda i: (i, 0)),
        out_shape=jax.ShapeDtypeStruct((t, e), F32),
        compiler_params=_cparams(("parallel",)),
        name="router_logits",
    )(x, w)


def _expert_kernel(be_ref, nv_ref, nb_ref, slot_ref, x_hbm, wg_ref, wu_ref, wd_ref, y_hbm,
                   xbuf, obuf, wgb_ref, wub_ref, wdb_ref, gsem, ssem, *, bm):
    i = pl.program_id(0)
    n_used = nb_ref[0]
    cur = lax.rem(i, 2)
    n_tok = x_hbm.shape[0]

    def token_of(slot):
        if TOP_K & (TOP_K - 1) == 0:
            return lax.shift_right_logical(slot, TOP_K.bit_length() - 1)
        return lax.div(slot, TOP_K)

    def start_gather(blk, buf):
        base = blk * bm
        for r in range(bm):
            tok = token_of(slot_ref[base + r])
            pltpu.make_async_copy(x_hbm.at[pl.ds(tok, 1), :], xbuf.at[buf, pl.ds(r, 1), :], gsem.at[buf]).start()

    def scatter_row(base, buf, r):
        slot = slot_ref[base + r]
        tok = token_of(slot)
        dst = (slot - tok * TOP_K) * n_tok + tok
        pltpu.make_async_copy(obuf.at[buf, pl.ds(r, 1), :], y_hbm.at[pl.ds(dst, 1), :], ssem.at[buf]).start()

    def start_scatter(blk, buf):
        base = blk * bm
        nv = nv_ref[blk]

        @pl.when(nv == bm)
        def _():
            for r in range(bm):
                scatter_row(base, buf, r)

        @pl.when(nv < bm)
        def _():
            def group(g, carry):
                r0 = pl.multiple_of(g * SUBLANES, SUBLANES)
                for j in range(SUBLANES):
                    scatter_row(base, buf, r0 + j)
                return carry
            full = lax.shift_right_logical(nv, SUBLANES.bit_length() - 1)
            lax.fori_loop(0, full, group, 0)

            def single(r, carry):
                scatter_row(base, buf, r)
                return carry
            lax.fori_loop(full * SUBLANES, nv, single, 0)

    def wait_scatter(blk, buf):
        n = nv_ref[blk]
        piece = bm
        while piece >= 1:
            @pl.when((n & piece) != 0)
            def _(piece=piece):
                rows, times = (piece, 1) if piece >= SUBLANES else (1, piece)
                for _ in range(times):
                    pltpu.make_async_copy(obuf.at[buf, pl.ds(0, rows), :], y_hbm.at[pl.ds(0, rows), :],
                                          ssem.at[buf]).wait()
            piece //= 2

    @pl.when(i == 0)
    def _():
        start_gather(0, 0)

    def wait_gather(buf):
        pltpu.make_async_copy(x_hbm.at[pl.ds(0, bm), :], xbuf.at[buf], gsem.at[buf]).wait()

    @pl.when(i < n_used)
    def _():
        wait_gather(cur)

        @pl.when(jnp.logical_or(i == 0, be_ref[i] != be_ref[jnp.maximum(i - 1, 0)]))
        def _():
            wgb_ref[...] = wg_ref[...].astype(BF16)
            wub_ref[...] = wu_ref[...].astype(BF16)
            wdb_ref[...] = wd_ref[...].astype(BF16)

        @pl.when(i >= 2)
        def _():
            wait_scatter(i - 2, cur)

        start_gather(jnp.minimum(i + 1, n_used - 1), 1 - cur)
        x = xbuf[cur].astype(BF16)
        g = _bdot(x, wgb_ref[...])
        h = (g * jax.nn.sigmoid(g)) * _bdot(x, wub_ref[...])
        obuf[cur] = _bdot(h.astype(BF16), wdb_ref[...])
        start_scatter(i, cur)

    @pl.when(i == pl.num_programs(0) - 1)
    def _():
        wait_gather(lax.rem(n_used, 2))

        @pl.when(n_used >= 2)
        def _():
            wait_scatter(n_used - 2, lax.rem(n_used, 2))
        wait_scatter(n_used - 1, lax.rem(n_used - 1, 2))


def _expert_mlp(block_e, n_valid, n_used, row_slot, x, w_gate, w_up, w_down, layer):
    t, d = x.shape
    ff = w_gate.shape[-1]
    bm = MOE_ROWS
    n_blocks = block_e.shape[0]
    grid_spec = pltpu.PrefetchScalarGridSpec(
        num_scalar_prefetch=4,
        grid=(n_blocks,),
        in_specs=[pl.BlockSpec(memory_space=pl.ANY),
                  pl.BlockSpec((None, None, d, ff), lambda i, be, nv, nb, sl: (layer, be[i], 0, 0)),
                  pl.BlockSpec((None, None, d, ff), lambda i, be, nv, nb, sl: (layer, be[i], 0, 0)),
                  pl.BlockSpec((None, None, ff, d), lambda i, be, nv, nb, sl: (layer, be[i], 0, 0))],
        out_specs=pl.BlockSpec(memory_space=pl.ANY),
        scratch_shapes=[pltpu.VMEM((2, bm, d), F32), pltpu.VMEM((2, bm, d), F32),
                        pltpu.VMEM((d, ff), BF16), pltpu.VMEM((d, ff), BF16), pltpu.VMEM((ff, d), BF16),
                        pltpu.SemaphoreType.DMA((2,)), pltpu.SemaphoreType.DMA((2,))],
    )
    return pl.pallas_call(
        functools.partial(_expert_kernel, bm=bm),
        grid_spec=grid_spec,
        out_shape=jax.ShapeDtypeStruct((TOP_K * t, d), F32),
        compiler_params=_cparams(("arbitrary",)),
        name="expert_mlp",
    )(block_e, n_valid, n_used, row_slot, x, w_gate, w_up, w_down)


def _first_argmax(x):
    return jnp.argmax(x, axis=-1).astype(jnp.int32)


def _moe(h, w_router, router_bias, w_gate, w_up, w_down, layer):
    t, d = h.shape
    ne = w_router.shape[1]
    ng = N_EXPERT_GROUPS
    eg = ne // ng
    scores = jax.nn.sigmoid(_router_logits(h, w_router))
    sel = scores + router_bias
    grp = sel.reshape(t, ng, eg)
    lane = jnp.arange(eg, dtype=jnp.int32)
    i1 = _first_argmax(grp)
    m1 = jnp.max(grp, axis=-1)
    m2 = jnp.max(jnp.where(lane == i1[..., None], -jnp.inf, grp), axis=-1)
    g_idx = _first_argmax(m1 + m2)
    sel_g = jnp.take_along_axis(grp, g_idx[:, None, None], axis=1)[:, 0]
    sc_g = jnp.take_along_axis(scores.reshape(t, ng, eg), g_idx[:, None, None], axis=1)[:, 0]
    e1 = _first_argmax(sel_g)
    e2 = _first_argmax(jnp.where(lane == e1[:, None], -jnp.inf, sel_g))
    w1 = jnp.take_along_axis(sc_g, e1[:, None], axis=1)[:, 0]
    w2 = jnp.take_along_axis(sc_g, e2[:, None], axis=1)[:, 0]
    wsum = w1 + w2
    e_idx = jnp.stack([g_idx * eg + e1, g_idx * eg + e2], axis=1)
    w = jnp.stack([w1 / wsum, w2 / wsum], axis=1)

    bm = MOE_ROWS
    n_assign = t * TOP_K
    flat_e = e_idx.reshape(-1)
    order = jnp.argsort(flat_e).astype(jnp.int32)
    counts = jnp.sum((flat_e[:, None] == jnp.arange(ne, dtype=jnp.int32)[None, :]).astype(jnp.int32), axis=0)
    starts = jnp.cumsum(counts) - counts
    pcounts = (counts + bm - 1) // bm * bm
    pends = jnp.cumsum(pcounts)
    pstarts = pends - pcounts
    n_blocks = -(-n_assign // bm) + ne
    blk_start = jnp.arange(n_blocks, dtype=jnp.int32) * bm
    block_e = jnp.minimum(jnp.sum((pends[None, :] <= blk_start[:, None]).astype(jnp.int32), axis=1), ne - 1)
    n_used = (pends[ne - 1:] // bm).astype(jnp.int32)
    blk_off = blk_start - pstarts[block_e]
    n_valid = jnp.clip(counts[block_e] - blk_off, 0, bm).astype(jnp.int32)
    off = blk_off[:, None] + jnp.arange(bm, dtype=jnp.int32)[None, :]
    src = jnp.clip(starts[block_e][:, None] + off, 0, n_assign - 1)
    row_slot = jnp.where(off < counts[block_e][:, None], order[src.reshape(-1)].reshape(n_blocks, bm), 0)
    y = _expert_mlp(block_e, n_valid, n_used, row_slot.reshape(-1), h, w_gate, w_up, w_down, layer)
    return y, w


def _rope_tables(pos, rope):
    inv = 1.0 / (ROPE_THETA ** (jnp.arange(0, rope, 2, dtype=F32) / rope))
    ang = pos.astype(F32)[:, None] * inv[None, :]
    cos, sin = jnp.cos(ang), jnp.sin(ang)
    pad = jnp.zeros((pos.shape[0], LANES - rope), F32)
    return jnp.concatenate([cos, cos, pad], axis=1), jnp.concatenate([-sin, sin, pad], axis=1)


def kernel(x_prompt, x_sample, c_prompt, c_sample, state_ssm_re, state_ssm_im, cache_ckv, cache_krope, w_ada, b_ada, ln_g, ln_b, ssm_a_re, ssm_a_im, ssm_log_dt, ssm_b_re, ssm_b_im, ssm_c_re, ssm_c_im, ssm_d, ssm_w_glu, mla_w_in, mla_q_norm, mla_kv_norm, mla_w_uq, mla_w_uk, mla_w_uv, mla_w_o, w_router, router_bias, moe_w_gate, moe_w_up, moe_w_down):
    depth, d = w_ada.shape[0], w_ada.shape[1]
    alpha = (2 * depth) ** 0.25
    nh, nope, vd = mla_w_uk.shape[2], mla_w_uk.shape[3], mla_w_uv.shape[3]
    ql, kvl = mla_q_norm.shape[1], mla_kv_norm.shape[1]
    rope = mla_w_uq.shape[3] - nope
    assert nope == LANES and vd == LANES and rope <= LANES
    scale = (nope + rope) ** -0.5 * math.log2(math.e)
    n_groups, n_state = ssm_a_re.shape[1], ssm_a_re.shape[2]
    past = cache_ckv.shape[2]

    xs = [x_prompt, x_sample]
    bsz = [x.shape[0] for x in xs]
    seq = [x.shape[1] for x in xs]
    row0 = [0, bsz[0] * seq[0]]
    n_tokens = row0[1] + bsz[1] * seq[1]
    pos = [jnp.arange(seq[0]), past + jnp.arange(seq[1])]
    tabs = [_rope_tables(p, rope) for p in pos]

    mod = _ada_mod(jnp.concatenate([c_prompt, c_sample], axis=0), w_ada, b_ada)

    def mod_rows(i, grp, piece):
        b0 = 0 if grp == 0 else bsz[0]
        return mod[i, b0:b0 + bsz[grp], piece * d:(piece + 1) * d].reshape(bsz[grp], 1, d)

    assert depth % 2 == 0, "layers alternate S5 / MLA mixers, starting with S5"
    w_router_b = w_router.astype(BF16)
    half = rope // 2
    ssm_out = [[[], []], [[], []]]
    ckv_out, kr_out = [[], []], [[], []]

    h_in = [None, None]
    for i in range(depth):
        j = i // 2
        x1s, tokens = [], jnp.zeros((n_tokens, d), F32)
        if i % 2 == 0:
            bd, cd, ab = _s5_tables(ssm_a_re[j], ssm_a_im[j], ssm_log_dt[j], ssm_b_re[j], ssm_b_im[j],
                                    ssm_c_re[j], ssm_c_im[j])
            w_glu = ssm_w_glu[j].astype(BF16)
        else:
            w_in = mla_w_in[j]
            r0 = ql + kvl
            zc = jnp.zeros((d, LANES - rope), F32)
            w_in_ext = jnp.concatenate([w_in, zc, w_in[:, r0 + half:], w_in[:, r0:r0 + half], zc], axis=1).astype(BF16)
            wq = mla_w_uq[j]
            zq = jnp.zeros((ql, nh, LANES - rope), F32)
            wq_rope = jnp.concatenate([wq[..., nope:], zq], axis=-1)
            wq_rot = jnp.concatenate([wq[..., nope + half:], wq[..., nope:nope + half], zq], axis=-1)
            wq = jnp.concatenate([wq[..., :nope].reshape(ql, -1), wq_rope.reshape(ql, -1),
                                  wq_rot.reshape(ql, -1)], axis=1).astype(BF16)
            wkv = jnp.concatenate([mla_w_uk[j].reshape(kvl, -1), mla_w_uv[j].reshape(kvl, -1)],
                                  axis=1).astype(BF16)
            w_ukt = mla_w_uk[j].transpose(1, 2, 0).astype(BF16)
            w_uvh = mla_w_uv[j].transpose(1, 0, 2).astype(BF16)
            w_o = mla_w_o[j].astype(BF16)
        for grp in range(2):
            x = xs[grp]
            b, l = bsz[grp], seq[grp]
            sh1, sc1, g1, sh2, sc2 = (mod_rows(i, grp, p) for p in range(5))
            lng, lnb = ln_g[i, 0].reshape(1, d), ln_b[i, 0].reshape(1, d)
            if i % 2 == 0:
                u = _modulate_to_s5_layout(x, sc1, sh1)
                nbg = b // SUBLANES
                if grp == 0:
                    h0r = jnp.zeros((nbg, SUBLANES, n_groups * n_state), F32)
                    h0i = h0r
                else:
                    h0r = state_ssm_re[j].reshape(nbg, SUBLANES, n_groups * n_state)
                    h0i = state_ssm_im[j].reshape(nbg, SUBLANES, n_groups * n_state)
                z, hr, hi = _s5_scan(u, bd, cd, ab, ssm_d[j].reshape(1, d), h0r, h0i)
                ssm_out[grp][0].append(hr.reshape(b, n_groups, n_state))
                ssm_out[grp][1].append(hi.reshape(b, n_groups, n_state))
                z = z.reshape(nbg, l, SUBLANES * d)

                def a_spec(tm):
                    return pl.BlockSpec((None, tm, d), lambda bi, t: (bi // SUBLANES, t, bi % SUBLANES))

                x1, tokens = _mix_ln(z, a_spec, w_glu, x, g1, lng, lnb, sc2, sh2, tokens, row0[grp],
                                     glu=True, alpha=alpha)
            else:
                cos_t, sin_t = tabs[grp]
                cq, ckv, kr, krp = _mla_in(h_in[grp], w_in_ext, mla_q_norm[j].reshape(1, ql),
                                           mla_kv_norm[j].reshape(1, kvl), cos_t, sin_t, ql=ql, kvl=kvl, rope=rope)
                q = _mla_q(cq, wq, cos_t, sin_t, nh=nh, scale=scale)
                if grp == 0:
                    k, v = _mla_kv(ckv, krp, wkv, nh=nh)
                    o = _prompt_attn(q, k, v, vd=vd)
                else:
                    q_lat = _absorb_q(q, w_ukt, nope=nope)
                    o_lat = _cached_attn(q_lat, q, cache_ckv[j], cache_krope[j], ckv, kr, nope=nope, rope=rope)
                    o = _absorb_v(o_lat, w_uvh)
                ckv_out[grp].append(ckv)
                kr_out[grp].append(kr)

                def a_spec(tm):
                    return pl.BlockSpec((None, tm, nh * vd), lambda bi, t: (bi, t, 0))

                x1, tokens = _mix_ln(o, a_spec, w_o, x, g1, lng, lnb, sc2, sh2, tokens, row0[grp],
                                     glu=False, alpha=alpha)
            x1s.append(x1)
        y_all, wt_all = _moe(tokens, w_router_b, router_bias, moe_w_gate, moe_w_up, moe_w_down, i)
        for grp in range(2):
            g2 = mod_rows(i, grp, 5)
            next_mod = (mod_rows(i + 1, grp, 1), mod_rows(i + 1, grp, 0)) if i % 2 == 0 else None
            xs[grp], h_in[grp] = _ffn_ln(y_all, wt_all, row0[grp], x1s[grp], g2, ln_g[i, 1].reshape(1, d),
                                         ln_b[i, 1].reshape(1, d), next_mod, alpha=alpha)

    return (xs[0], xs[1],
            jnp.stack(ssm_out[0][0]), jnp.stack(ssm_out[0][1]), jnp.stack(ckv_out[0]), jnp.stack(kr_out[0]),
            jnp.stack(ssm_out[1][0]), jnp.stack(ssm_out[1][1]), jnp.stack(ckv_out[1]), jnp.stack(kr_out[1]))
```

```python
import functools
import math

import jax
import jax.numpy as jnp
from jax import lax
from jax.experimental import pallas as pl
from jax.experimental.pallas import tpu as pltpu

F32 = jnp.float32
BF16 = jnp.bfloat16

CHUNK = 64
SSM_GROUP_WIDTH = 16
N_EXPERT_GROUPS = 8
TOP_K = 2
ROPE_THETA = 10000.0
LN_EPS = 1e-5
RMS_EPS = 1e-6
NEG_INF = -1e30

V7X_VMEM_BYTES = 64 * 1024 * 1024
SUBLANES = 8
LANES = 128
MXU_DIM = 256

VMEM_LIMIT = 56 * 1024 * 1024
S5_CHANNELS = MXU_DIM
S5_TIME = 128
ROW_TILE = 256
ATTN_BLOCK = 512
ATTN_HEADS = 4
MOE_ROWS = 256
CACHE_KEYS = 1024


def _cparams(sem):
    return pltpu.CompilerParams(dimension_semantics=sem, vmem_limit_bytes=VMEM_LIMIT)


def _bdot(a, b):
    return jnp.dot(a, b, preferred_element_type=F32)


def _ada_kernel(c_ref, w_ref, b_ref, o_ref):
    c = c_ref[...]
    cs = (c * jax.nn.sigmoid(c)).astype(BF16)
    o_ref[...] = _bdot(cs, w_ref[...].astype(BF16)) + b_ref[...]


def _ada_mod(c_all, w_ada, b_ada):
    depth, d, n = w_ada.shape
    bc = c_all.shape[0]
    tn = 1024
    return pl.pallas_call(
        _ada_kernel,
        grid=(depth, n // tn),
        in_specs=[pl.BlockSpec((bc, d), lambda i, j: (0, 0)),
                  pl.BlockSpec((None, d, tn), lambda i, j: (i, 0, j)),
                  pl.BlockSpec((None, 1, tn), lambda i, j: (i, 0, j))],
        out_specs=pl.BlockSpec((None, bc, tn), lambda i, j: (i, 0, j)),
        out_shape=jax.ShapeDtypeStruct((depth, bc, n), F32),
        compiler_params=_cparams(("parallel", "parallel")),
        name="ada_mod",
    )(c_all, w_ada, b_ada.reshape(depth, 1, n))


def _s5_kernel(x_ref, sc_ref, sh_ref, bd_ref, cd_ref, ab_ref, d_ref, h0r_ref, h0i_ref,
               z_ref, hr_ref, hi_ref, u_ref, xs_ref, hs_ref, h_ref, *, tt, sb):
    t = pl.program_id(2)

    @pl.when(t == 0)
    def _():
        h_ref[0] = h0r_ref[...]
        h_ref[1] = h0i_ref[...]

    lane_tiles = range(u_ref.shape[0])
    for b in range(SUBLANES):
        ub = x_ref[b] * (1.0 + sc_ref[b]) + sh_ref[b]
        for c in lane_tiles:
            u_ref[c, pl.ds(b, tt, stride=SUBLANES), :] = ub[:, c * LANES:(c + 1) * LANES]
    u = jnp.concatenate([u_ref[c] for c in lane_tiles], axis=1)
    xs_ref[...] = _bdot(u.astype(BF16), bd_ref[...])
    a_re = jnp.broadcast_to(ab_ref[0:1, :], (SUBLANES, sb))
    a_im = jnp.broadcast_to(ab_ref[1:2, :], (SUBLANES, sb))

    pair = 2 * SUBLANES

    def step(i, carry):
        h_re, h_im = carry
        r0 = pl.multiple_of(i * pair, pair)
        res, ims = [], []
        for s in range(2):
            x_re = xs_ref[pl.ds(r0 + s * SUBLANES, SUBLANES), 0:sb]
            x_im = xs_ref[pl.ds(r0 + s * SUBLANES, SUBLANES), sb:2 * sb]
            h_re, h_im = a_re * h_re - a_im * h_im + x_re, a_re * h_im + a_im * h_re + x_im
            res.append(h_re)
            ims.append(h_im)
        hs_ref[pl.ds(r0, pair), 0:sb] = jnp.concatenate(res, axis=0).astype(BF16)
        hs_ref[pl.ds(r0, pair), sb:2 * sb] = jnp.concatenate(ims, axis=0).astype(BF16)
        return h_re, h_im

    h_re, h_im = lax.fori_loop(0, tt // 2, step, (h_ref[0], h_ref[1]))
    h_ref[0] = h_re
    h_ref[1] = h_im
    y = _bdot(hs_ref[...], cd_ref[...]) + d_ref[...] * u
    zf = jax.nn.gelu(y)
    for c in lane_tiles:
        u_ref[c] = zf[:, c * LANES:(c + 1) * LANES]
    for b in range(SUBLANES):
        for c in lane_tiles:
            z_ref[b, :, c * LANES:(c + 1) * LANES] = (
                u_ref[c, pl.ds(b, tt, stride=SUBLANES), :].astype(z_ref.dtype))

    @pl.when(t == pl.num_programs(2) - 1)
    def _():
        hr_ref[...] = h_re
        hi_ref[...] = h_im


def _s5_scan(x, sc, sh, bd, cd, ab, d_skip, h0_re, h0_im):
    b, l, d = x.shape
    nbg = b // SUBLANES
    nj, cb, sb2 = bd.shape
    sb = sb2 // 2
    tt = min(l, S5_TIME)
    kern = functools.partial(_s5_kernel, tt=tt, sb=sb)
    tile = pl.BlockSpec((SUBLANES, tt, cb), lambda g, j, t: (g, t, j))
    row = pl.BlockSpec((SUBLANES, 1, cb), lambda g, j, t: (g, 0, j))
    state_spec = pl.BlockSpec((None, SUBLANES, sb), lambda g, j, t: (g, 0, j))
    state_shape = jax.ShapeDtypeStruct((nbg, SUBLANES, nj * sb), F32)
    return pl.pallas_call(
        kern,
        grid=(nbg, nj, l // tt),
        in_specs=[tile, row, row,
                  pl.BlockSpec((None, cb, sb2), lambda g, j, t: (j, 0, 0)),
                  pl.BlockSpec((None, sb2, cb), lambda g, j, t: (j, 0, 0)),
                  pl.BlockSpec((None, 2, sb), lambda g, j, t: (j, 0, 0)),
                  pl.BlockSpec((1, cb), lambda g, j, t: (0, j)),
                  state_spec, state_spec],
        out_specs=[tile, state_spec, state_spec],
        out_shape=[jax.ShapeDtypeStruct((b, l, d), BF16), state_shape, state_shape],
        scratch_shapes=[pltpu.VMEM((cb // LANES, tt * SUBLANES, LANES), F32), pltpu.VMEM((tt * SUBLANES, sb2), F32),
                        pltpu.VMEM((tt * SUBLANES, sb2), BF16), pltpu.VMEM((2, SUBLANES, sb), F32)],
        compiler_params=_cparams(("parallel", "parallel", "arbitrary")),
        name="s5_scan",
    )(x, sc, sh, bd, cd, ab, d_skip, h0_re, h0_im)


def _s5_tables(a_re, a_im, log_dt, b_re, b_im, c_re, c_im):
    g, p = a_re.shape
    w = b_re.shape[-1]
    gl = S5_CHANNELS // w
    nj = g // gl
    dt = jnp.exp(log_dt)[:, None]
    mag = jnp.exp(a_re * dt)
    abar_re, abar_im = mag * jnp.cos(a_im * dt), mag * jnp.sin(a_im * dt)
    den = a_re * a_re + a_im * a_im
    n_re, n_im = abar_re - 1.0, abar_im
    f_re = (n_re * a_re + n_im * a_im) / den
    f_im = (n_im * a_re - n_re * a_im) / den
    bb_re = f_re[..., None] * b_re - f_im[..., None] * b_im
    bb_im = f_re[..., None] * b_im + f_im[..., None] * b_re
    eye = jnp.eye(gl, dtype=F32)
    bb = jnp.stack([bb_re, bb_im]).reshape(2, nj, gl, p, w)
    bd = jnp.einsum('sjgpw,gh->jgwshp', bb, eye).reshape(nj, gl * w, 2 * gl * p)
    cc = jnp.stack([c_re, -c_im]).reshape(2, nj, gl, w, p)
    cd = jnp.einsum('sjgwp,gh->jsgphw', cc, eye).reshape(nj, 2 * gl * p, gl * w)
    ab = jnp.stack([abar_re.reshape(nj, gl * p), abar_im.reshape(nj, gl * p)], axis=1)
    return bd.astype(BF16), cd.astype(BF16), ab


def _layer_norm(y, g, b):
    mu = jnp.mean(y, axis=-1, keepdims=True)
    yc = y - mu
    var = jnp.mean(yc * yc, axis=-1, keepdims=True)
    return yc * lax.rsqrt(var + LN_EPS) * g + b


def _mix_ln_kernel(a_ref, w_ref, x_ref, gate_ref, lng_ref, lnb_ref, sc_ref, sh_ref, tokens_hbm,
                   x1_ref, h_ref, *, glu, alpha):
    del tokens_hbm
    acc = _bdot(a_ref[...], w_ref[...])
    if glu:
        d = acc.shape[-1] // 2
        m = acc[:, :d] * jax.nn.sigmoid(acc[:, d:])
    else:
        m = acc
    x1 = _layer_norm(alpha * x_ref[...] + gate_ref[...] * m, lng_ref[...], lnb_ref[...])
    x1_ref[...] = x1
    h_ref[...] = x1 * (1.0 + sc_ref[...]) + sh_ref[...]


def _mix_ln(a, a_spec, w, x, gate, ln_g, ln_b, sc, sh, tokens, row0, *, glu, alpha):
    b, l, d = x.shape
    tm = min(l, ROW_TILE)
    nt = l // tm
    blk0 = row0 // tm
    k, n = w.shape
    row = pl.BlockSpec((None, 1, d), lambda i, t: (i, 0, 0))
    vec = pl.BlockSpec((1, d), lambda i, t: (0, 0))
    tile = pl.BlockSpec((None, tm, d), lambda i, t: (i, t, 0))
    return pl.pallas_call(
        functools.partial(_mix_ln_kernel, glu=glu, alpha=alpha),
        grid=(b, nt),
        in_specs=[a_spec(tm), pl.BlockSpec((k, n), lambda i, t: (0, 0), pipeline_mode=pl.Buffered(1)),
                  tile, row, vec, vec, row, row, pl.BlockSpec(memory_space=pl.ANY)],
        out_specs=[tile, pl.BlockSpec((tm, d), lambda i, t: (blk0 + i * nt + t, 0))],
        out_shape=[jax.ShapeDtypeStruct((b, l, d), F32), jax.ShapeDtypeStruct(tokens.shape, F32)],
        input_output_aliases={8: 1},
        compiler_params=_cparams(("parallel", "parallel")),
        name="mix_ln_glu" if glu else "mix_ln_proj",
    )(a, w, x, gate, ln_g, ln_b, sc, sh, tokens)


def _ffn_ln_kernel(*refs, alpha):
    y_refs, (wt_ref, x_ref, gate_ref, lng_ref, lnb_ref), rest = refs[:TOP_K], refs[TOP_K:TOP_K + 5], refs[TOP_K + 5:]
    f = wt_ref[:, 0:1] * y_refs[0][...]
    for k in range(1, TOP_K):
        f = f + wt_ref[:, k:k + 1] * y_refs[k][...]
    x2 = _layer_norm(alpha * x_ref[...] + gate_ref[...] * f, lng_ref[...], lnb_ref[...])
    if len(rest) == 1:
        rest[0][...] = x2
    else:
        sc_ref, sh_ref, x2_ref, h_ref = rest
        x2_ref[...] = x2
        h_ref[...] = (x2 * (1.0 + sc_ref[...]) + sh_ref[...]).astype(h_ref.dtype)


def _ffn_ln(y_all, wt_all, row0, x, gate, ln_g, ln_b, next_mod, *, alpha):
    b, l, d = x.shape
    tm = min(l, ROW_TILE)
    nt = l // tm
    blk0 = row0 // tm
    kblk = wt_all.shape[0] // tm
    row = pl.BlockSpec((None, 1, d), lambda i, t: (i, 0, 0))
    vec = pl.BlockSpec((1, d), lambda i, t: (0, 0))
    tile = pl.BlockSpec((None, tm, d), lambda i, t: (i, t, 0))
    in_specs = [pl.BlockSpec((tm, d), functools.partial(lambda k, i, t: (k * kblk + blk0 + i * nt + t, 0), k))
                for k in range(TOP_K)]
    in_specs += [pl.BlockSpec((tm, TOP_K), lambda i, t: (blk0 + i * nt + t, 0)), tile, row, vec, vec]
    x2_shape = jax.ShapeDtypeStruct((b, l, d), F32)
    if next_mod is None:
        args, out_specs, out_shape = (), tile, x2_shape
    else:
        args, in_specs = tuple(next_mod), in_specs + [row, row]
        out_specs, out_shape = [tile, tile], [x2_shape, jax.ShapeDtypeStruct((b, l, d), BF16)]
    out = pl.pallas_call(
        functools.partial(_ffn_ln_kernel, alpha=alpha),
        grid=(b, nt),
        in_specs=in_specs,
        out_specs=out_specs,
        out_shape=out_shape,
        compiler_params=_cparams(("parallel", "parallel")),
        name="ffn_ln",
    )(*([y_all] * TOP_K), wt_all, x, gate, ln_g, ln_b, *args)
    return (out, None) if next_mod is None else out


def _rms_norm(x, g):
    ms = jnp.mean(x * x, axis=-1, keepdims=True)
    return x * lax.rsqrt(ms + RMS_EPS) * g


def _mla_in_kernel(h_ref, w_ref, qn_ref, kvn_ref, cos_ref, sin_ref,
                   cq_ref, ckv_ref, kr_ref, krp_ref, *, ql, kvl, rope):
    acc = _bdot(h_ref[...], w_ref[...])
    cq_ref[...] = _rms_norm(acc[:, :ql], qn_ref[...]).astype(cq_ref.dtype)
    ckv_ref[...] = _rms_norm(acc[:, ql:ql + kvl], kvn_ref[...])
    o = ql + kvl
    kr = acc[:, o:o + LANES] * cos_ref[...] + acc[:, o + LANES:o + 2 * LANES] * sin_ref[...]
    kr_ref[...] = kr[:, :rope]
    krp_ref[...] = kr.astype(krp_ref.dtype)


def _mla_in(h, w_in_ext, q_norm, kv_norm, cos_t, sin_t, *, ql, kvl, rope):
    b, l, d = h.shape
    tm = min(l, ROW_TILE)
    n = w_in_ext.shape[1]
    kern = functools.partial(_mla_in_kernel, ql=ql, kvl=kvl, rope=rope)
    tab = pl.BlockSpec((tm, LANES), lambda i, t: (t, 0))

    def out(nf):
        return pl.BlockSpec((None, tm, nf), lambda i, t: (i, t, 0))

    return pl.pallas_call(
        kern,
        grid=(b, l // tm),
        in_specs=[pl.BlockSpec((None, tm, d), lambda i, t: (i, t, 0)),
                  pl.BlockSpec((d, n), lambda i, t: (0, 0)),
                  pl.BlockSpec((1, ql), lambda i, t: (0, 0)),
                  pl.BlockSpec((1, kvl), lambda i, t: (0, 0)), tab, tab],
        out_specs=[out(ql), out(kvl), out(rope), out(LANES)],
        out_shape=[jax.ShapeDtypeStruct((b, l, ql), BF16), jax.ShapeDtypeStruct((b, l, kvl), F32),
                   jax.ShapeDtypeStruct((b, l, rope), F32), jax.ShapeDtypeStruct((b, l, LANES), BF16)],
        compiler_params=_cparams(("parallel", "parallel")),
        name="mla_in",
    )(h, w_in_ext, q_norm, kv_norm, cos_t, sin_t)


def _mla_q_kernel(cq_ref, w_ref, cos_ref, sin_ref, q_ref, *, nh, scale):
    acc = _bdot(cq_ref[...], w_ref[...])
    cos = cos_ref[...] * scale
    sin = sin_ref[...] * scale
    for h in range(nh):
        q_ref[h, :, :LANES] = (acc[:, h * LANES:(h + 1) * LANES] * scale).astype(q_ref.dtype)
        r0, p0 = (nh + h) * LANES, (2 * nh + h) * LANES
        q_ref[h, :, LANES:] = (acc[:, r0:r0 + LANES] * cos + acc[:, p0:p0 + LANES] * sin).astype(q_ref.dtype)


def _mla_q(cq, wq, cos_t, sin_t, *, nh, scale):
    b, l, ql = cq.shape
    n = wq.shape[1]
    tm = min(l, ROW_TILE)
    tab = pl.BlockSpec((tm, LANES), lambda i, t: (t, 0))
    return pl.pallas_call(
        functools.partial(_mla_q_kernel, nh=nh, scale=scale),
        grid=(b, l // tm),
        in_specs=[pl.BlockSpec((None, tm, ql), lambda i, t: (i, t, 0)),
                  pl.BlockSpec((ql, n), lambda i, t: (0, 0)), tab, tab],
        out_specs=pl.BlockSpec((None, nh, tm, 2 * LANES), lambda i, t: (i, 0, t, 0)),
        out_shape=jax.ShapeDtypeStruct((b, nh, l, 2 * LANES), BF16),
        compiler_params=_cparams(("parallel", "parallel")),
        name="mla_q",
    )(cq, wq, cos_t, sin_t)


def _mla_kv_kernel(ckv_ref, krp_ref, w_ref, k_ref, v_ref, *, nh):
    acc = _bdot(ckv_ref[...].astype(BF16), w_ref[...])
    krp = krp_ref[...]
    lane = lax.broadcasted_iota(jnp.int32, krp.shape, 1)
    ones_col = jnp.where(lane == 0, 1.0, 0.0).astype(v_ref.dtype)
    for h in range(nh):
        k_ref[h, :, :LANES] = acc[:, h * LANES:(h + 1) * LANES].astype(k_ref.dtype)
        k_ref[h, :, LANES:] = krp
        v_ref[h, :, :LANES] = acc[:, (nh + h) * LANES:(nh + h + 1) * LANES].astype(v_ref.dtype)
        v_ref[h, :, LANES:] = ones_col


def _mla_kv(ckv, krp, wkv, *, nh):
    b, l, kvl = ckv.shape
    n = wkv.shape[1]
    tm = min(l, ROW_TILE)
    head_tile = pl.BlockSpec((None, nh, tm, 2 * LANES), lambda i, t: (i, 0, t, 0))
    head_shape = jax.ShapeDtypeStruct((b, nh, l, 2 * LANES), BF16)
    return pl.pallas_call(
        functools.partial(_mla_kv_kernel, nh=nh),
        grid=(b, l // tm),
        in_specs=[pl.BlockSpec((None, tm, kvl), lambda i, t: (i, t, 0)),
                  pl.BlockSpec((None, tm, LANES), lambda i, t: (i, t, 0)),
                  pl.BlockSpec((kvl, n), lambda i, t: (0, 0))],
        out_specs=[head_tile, head_tile],
        out_shape=[head_shape, head_shape],
        compiler_params=_cparams(("parallel", "parallel")),
        name="mla_kv",
    )(ckv, krp, wkv)


def _qk(q, k):
    return lax.dot_general(q, k, (((1,), (1,)), ((), ())), preferred_element_type=F32)


def _softmax_step(carry, s, v):
    m, l, acc = carry
    m_new = jnp.maximum(m, jnp.max(s, axis=-1, keepdims=True))
    p = jnp.exp2(s - m_new)
    alpha = jnp.exp2(m - m_new)
    l = alpha * l + jnp.sum(p, axis=-1, keepdims=True)
    acc = alpha * acc + _bdot(p.astype(BF16), v)
    return m_new, l, acc


def _softmax_step_ones(carry, s, v):
    m, acc = carry
    m_new = jnp.maximum(m, jnp.max(s, axis=-1, keepdims=True))
    p = jnp.exp2(s - m_new)
    acc = jnp.exp2(m - m_new) * acc + _bdot(p.astype(BF16), v)
    return m_new, acc


def _prompt_attn_kernel(q_ref, k_ref, v_ref, o_ref, *, blk, vd):
    i = pl.program_id(2)
    heads = range(q_ref.shape[0])
    qs = [q_ref[h] for h in heads]

    def body(j, carry):
        r0 = pl.multiple_of(j * blk, blk)
        return tuple(_softmax_step_ones(carry[h], _qk(qs[h], k_ref[h, pl.ds(r0, blk), :]),
                                        v_ref[h, pl.ds(r0, blk), :]) for h in heads)

    init = (jnp.full((blk, 1), NEG_INF, F32), jnp.zeros((blk, v_ref.shape[-1]), F32))
    carry = lax.fori_loop(0, i, body, tuple(init for _ in heads))
    r0 = pl.multiple_of(i * blk, blk)
    q_chunk = lax.broadcasted_iota(jnp.int32, (blk, blk), 0) // CHUNK
    k_chunk = lax.broadcasted_iota(jnp.int32, (blk, blk), 1) // CHUNK
    visible = k_chunk <= q_chunk
    for h in heads:
        s = jnp.where(visible, _qk(qs[h], k_ref[h, pl.ds(r0, blk), :]), NEG_INF)
        _, acc = _softmax_step_ones(carry[h], s, v_ref[h, pl.ds(r0, blk), :])
        o_ref[:, h * vd:(h + 1) * vd] = (acc[:, :vd] / acc[:, vd:vd + 1]).astype(o_ref.dtype)


def _prompt_attn(q, k, v, *, vd):
    b, nh, l, dk = q.shape
    blk = min(l, ATTN_BLOCK)
    hp = ATTN_HEADS
    return pl.pallas_call(
        functools.partial(_prompt_attn_kernel, blk=blk, vd=vd),
        grid=(b, nh // hp, l // blk),
        in_specs=[pl.BlockSpec((None, hp, blk, dk), lambda i, h, t: (i, h, t, 0)),
                  pl.BlockSpec((None, hp, l, dk), lambda i, h, t: (i, h, 0, 0)),
                  pl.BlockSpec((None, hp, l, v.shape[-1]), lambda i, h, t: (i, h, 0, 0))],
        out_specs=pl.BlockSpec((None, blk, hp * vd), lambda i, h, t: (i, t, h)),
        out_shape=jax.ShapeDtypeStruct((b, l, nh * vd), BF16),
        compiler_params=_cparams(("parallel", "parallel", "arbitrary")),
        name="prompt_attn",
    )(q, k, v)


def _absorb_q_kernel(q_ref, w_ref, o_ref, *, nope):
    b, lq, dk = q_ref.shape
    q = q_ref[...].reshape(b * lq, dk)[:, :nope]
    o_ref[...] = _bdot(q, w_ref[...]).reshape(b, lq, -1).astype(o_ref.dtype)


def _absorb_q(q, w_ukt, *, nope):
    b, nh, lq, dk = q.shape
    kvl = w_ukt.shape[-1]
    return pl.pallas_call(
        functools.partial(_absorb_q_kernel, nope=nope),
        grid=(nh,),
        in_specs=[pl.BlockSpec((b, None, lq, dk), lambda h: (0, h, 0, 0)),
                  pl.BlockSpec((None, nope, kvl), lambda h: (h, 0, 0))],
        out_specs=pl.BlockSpec((b, None, lq, kvl), lambda h: (0, h, 0, 0)),
        out_shape=jax.ShapeDtypeStruct((b, nh, lq, kvl), BF16),
        compiler_params=_cparams(("parallel",)),
        name="absorb_q",
    )(q, w_ukt)


def _cached_attn_kernel(ql_ref, q_ref, cc_ref, ck_ref, nc_ref, nk_ref, o_ref, *, nope, rope, past, kb):
    nh, lq, kvl = ql_ref.shape
    rows = nh * lq
    ql = ql_ref[...].reshape(rows, kvl)
    qr = q_ref[...].reshape(rows, -1)[:, nope:nope + rope]
    q_pos = past + lax.broadcasted_iota(jnp.int32, (rows, 1), 0) % lq
    carry = (jnp.full((rows, 1), NEG_INF, F32), jnp.zeros((rows, 1), F32), jnp.zeros((rows, kvl), F32))

    def scores(c, kr, k0):
        s = _qk(ql, c) + _qk(qr, kr)
        k_pos = k0 + lax.broadcasted_iota(jnp.int32, s.shape, 1)
        return jnp.where(k_pos // CHUNK <= q_pos // CHUNK, s, NEG_INF)

    for j in range(past // kb):
        c = cc_ref[j * kb:(j + 1) * kb, :].astype(BF16)
        kr = ck_ref[j * kb:(j + 1) * kb, :].astype(BF16)
        carry = _softmax_step(carry, scores(c, kr, j * kb), c)
    c = nc_ref[...].astype(BF16)
    kr = nk_ref[...].astype(BF16)
    _, l, acc = _softmax_step(carry, scores(c, kr, past), c)
    o_ref[...] = (acc / l).reshape(nh, lq, kvl).astype(o_ref.dtype)


def _cached_attn(q_lat, q, cache_ckv, cache_kr, new_ckv, new_kr, *, nope, rope):
    b, nh, lq, kvl = q_lat.shape
    past = cache_ckv.shape[1]
    kb = min(past, CACHE_KEYS)
    dk = q.shape[-1]
    kern = functools.partial(_cached_attn_kernel, nope=nope, rope=rope, past=past, kb=kb)
    return pl.pallas_call(
        kern,
        grid=(b,),
        in_specs=[pl.BlockSpec((None, nh, lq, kvl), lambda i: (i, 0, 0, 0)),
                  pl.BlockSpec((None, nh, lq, dk), lambda i: (i, 0, 0, 0)),
                  pl.BlockSpec((None, past, kvl), lambda i: (i, 0, 0)),
                  pl.BlockSpec((None, past, rope), lambda i: (i, 0, 0)),
                  pl.BlockSpec((None, lq, kvl), lambda i: (i, 0, 0)),
                  pl.BlockSpec((None, lq, rope), lambda i: (i, 0, 0))],
        out_specs=pl.BlockSpec((None, nh, lq, kvl), lambda i: (i, 0, 0, 0)),
        out_shape=jax.ShapeDtypeStruct((b, nh, lq, kvl), BF16),
        compiler_params=_cparams(("parallel",)),
        name="cached_attn",
    )(q_lat, q, cache_ckv, cache_kr, new_ckv, new_kr)


def _absorb_v_kernel(o_ref, w_ref, out_ref):
    b, lq, kvl = o_ref.shape
    out_ref[...] = _bdot(o_ref[...].reshape(b * lq, kvl), w_ref[...]).reshape(b, lq, -1).astype(out_ref.dtype)


def _absorb_v(o_lat, w_uv):
    b, nh, lq, kvl = o_lat.shape
    vd = w_uv.shape[-1]
    return pl.pallas_call(
        _absorb_v_kernel,
        grid=(nh,),
        in_specs=[pl.BlockSpec((b, None, lq, kvl), lambda h: (0, h, 0, 0)),
                  pl.BlockSpec((None, kvl, vd), lambda h: (h, 0, 0))],
        out_specs=pl.BlockSpec((b, lq, vd), lambda h: (0, 0, h)),
        out_shape=jax.ShapeDtypeStruct((b, lq, nh * vd), BF16),
        compiler_params=_cparams(("parallel",)),
        name="absorb_v",
    )(o_lat, w_uv)


def _router_kernel(x_ref, w_ref, o_ref):
    o_ref[...] = _bdot(x_ref[...].astype(BF16), w_ref[...])


def _router_logits(x, w):
    t, d = x.shape
    e = w.shape[1]
    tm = math.gcd(t, 1024)
    return pl.pallas_call(
        _router_kernel,
        grid=(t // tm,),
        in_specs=[pl.BlockSpec((tm, d), lambda i: (i, 0)), pl.BlockSpec((d, e), lambda i: (0, 0))],
        out_specs=pl.BlockSpec((tm, e), lambda i: (i, 0)),
        out_shape=jax.ShapeDtypeStruct((t, e), F32),
        compiler_params=_cparams(("parallel",)),
        name="router_logits",
    )(x, w)


def _expert_kernel(be_ref, nv_ref, nb_ref, slot_ref, x_hbm, wg_ref, wu_ref, wd_ref, y_hbm,
                   xbuf, obuf, wgb_ref, wub_ref, wdb_ref, gsem, ssem, *, bm):
    i = pl.program_id(0)
    n_used = nb_ref[0]
    cur = lax.rem(i, 2)
    n_tok = x_hbm.shape[0]

    def token_of(slot):
        if TOP_K & (TOP_K - 1) == 0:
            return lax.shift_right_logical(slot, TOP_K.bit_length() - 1)
        return lax.div(slot, TOP_K)

    def start_gather(blk, buf):
        base = blk * bm
        for r in range(bm):
            tok = token_of(slot_ref[base + r])
            pltpu.make_async_copy(x_hbm.at[pl.ds(tok, 1), :], xbuf.at[buf, pl.ds(r, 1), :], gsem.at[buf]).start()

    def scatter_row(base, buf, r):
        slot = slot_ref[base + r]
        tok = token_of(slot)
        dst = (slot - tok * TOP_K) * n_tok + tok
        pltpu.make_async_copy(obuf.at[buf, pl.ds(r, 1), :], y_hbm.at[pl.ds(dst, 1), :], ssem.at[buf]).start()

    def start_scatter(blk, buf):
        base = blk * bm
        nv = nv_ref[blk]

        @pl.when(nv == bm)
        def _():
            for r in range(bm):
                scatter_row(base, buf, r)

        @pl.when(nv < bm)
        def _():
            def group(g, carry):
                r0 = pl.multiple_of(g * SUBLANES, SUBLANES)
                for j in range(SUBLANES):
                    scatter_row(base, buf, r0 + j)
                return carry
            full = lax.shift_right_logical(nv, SUBLANES.bit_length() - 1)
            lax.fori_loop(0, full, group, 0)

            def single(r, carry):
                scatter_row(base, buf, r)
                return carry
            lax.fori_loop(full * SUBLANES, nv, single, 0)

    def wait_scatter(blk, buf):
        n = nv_ref[blk]
        piece = bm
        while piece >= 1:
            @pl.when((n & piece) != 0)
            def _(piece=piece):
                rows, times = (piece, 1) if piece >= SUBLANES else (1, piece)
                for _ in range(times):
                    pltpu.make_async_copy(obuf.at[buf, pl.ds(0, rows), :], y_hbm.at[pl.ds(0, rows), :],
                                          ssem.at[buf]).wait()
            piece //= 2

    @pl.when(i == 0)
    def _():
        start_gather(0, 0)

    def wait_gather(buf):
        pltpu.make_async_copy(x_hbm.at[pl.ds(0, bm), :], xbuf.at[buf], gsem.at[buf]).wait()

    @pl.when(i < n_used)
    def _():
        wait_gather(cur)

        @pl.when(jnp.logical_or(i == 0, be_ref[i] != be_ref[jnp.maximum(i - 1, 0)]))
        def _():
            wgb_ref[...] = wg_ref[...].astype(BF16)
            wub_ref[...] = wu_ref[...].astype(BF16)
            wdb_ref[...] = wd_ref[...].astype(BF16)

        @pl.when(i >= 2)
        def _():
            wait_scatter(i - 2, cur)

        start_gather(jnp.minimum(i + 1, n_used - 1), 1 - cur)
        x = xbuf[cur].astype(BF16)
        g = _bdot(x, wgb_ref[...])
        h = (g * jax.nn.sigmoid(g)) * _bdot(x, wub_ref[...])
        obuf[cur] = _bdot(h.astype(BF16), wdb_ref[...])
        start_scatter(i, cur)

    @pl.when(i == pl.num_programs(0) - 1)
    def _():
        wait_gather(lax.rem(n_used, 2))

        @pl.when(n_used >= 2)
        def _():
            wait_scatter(n_used - 2, lax.rem(n_used, 2))
        wait_scatter(n_used - 1, lax.rem(n_used - 1, 2))


def _expert_mlp(block_e, n_valid, n_used, row_slot, x, w_gate, w_up, w_down, layer):
    t, d = x.shape
    ff = w_gate.shape[-1]
    bm = MOE_ROWS
    n_blocks = block_e.shape[0]
    grid_spec = pltpu.PrefetchScalarGridSpec(
        num_scalar_prefetch=4,
        grid=(n_blocks,),
        in_specs=[pl.BlockSpec(memory_space=pl.ANY),
                  pl.BlockSpec((None, None, d, ff), lambda i, be, nv, nb, sl: (layer, be[i], 0, 0)),
                  pl.BlockSpec((None, None, d, ff), lambda i, be, nv, nb, sl: (layer, be[i], 0, 0)),
                  pl.BlockSpec((None, None, ff, d), lambda i, be, nv, nb, sl: (layer, be[i], 0, 0))],
        out_specs=pl.BlockSpec(memory_space=pl.ANY),
        scratch_shapes=[pltpu.VMEM((2, bm, d), F32), pltpu.VMEM((2, bm, d), F32),
                        pltpu.VMEM((d, ff), BF16), pltpu.VMEM((d, ff), BF16), pltpu.VMEM((ff, d), BF16),
                        pltpu.SemaphoreType.DMA((2,)), pltpu.SemaphoreType.DMA((2,))],
    )
    return pl.pallas_call(
        functools.partial(_expert_kernel, bm=bm),
        grid_spec=grid_spec,
        out_shape=jax.ShapeDtypeStruct((TOP_K * t, d), F32),
        compiler_params=_cparams(("arbitrary",)),
        name="expert_mlp",
    )(block_e, n_valid, n_used, row_slot, x, w_gate, w_up, w_down)


def _first_argmax(x):
    return jnp.argmax(x, axis=-1).astype(jnp.int32)


def _moe(h, w_router, router_bias, w_gate, w_up, w_down, layer):
    t, d = h.shape
    ne = w_router.shape[1]
    ng = N_EXPERT_GROUPS
    eg = ne // ng
    scores = jax.nn.sigmoid(_router_logits(h, w_router))
    sel = scores + router_bias
    grp = sel.reshape(t, ng, eg)
    lane = jnp.arange(eg, dtype=jnp.int32)
    i1 = _first_argmax(grp)
    m1 = jnp.max(grp, axis=-1)
    m2 = jnp.max(jnp.where(lane == i1[..., None], -jnp.inf, grp), axis=-1)
    g_idx = _first_argmax(m1 + m2)
    sel_g = jnp.take_along_axis(grp, g_idx[:, None, None], axis=1)[:, 0]
    sc_g = jnp.take_along_axis(scores.reshape(t, ng, eg), g_idx[:, None, None], axis=1)[:, 0]
    e1 = _first_argmax(sel_g)
    e2 = _first_argmax(jnp.where(lane == e1[:, None], -jnp.inf, sel_g))
    w1 = jnp.take_along_axis(sc_g, e1[:, None], axis=1)[:, 0]
    w2 = jnp.take_along_axis(sc_g, e2[:, None], axis=1)[:, 0]
    wsum = w1 + w2
    e_idx = jnp.stack([g_idx * eg + e1, g_idx * eg + e2], axis=1)
    w = jnp.stack([w1 / wsum, w2 / wsum], axis=1)

    bm = MOE_ROWS
    n_assign = t * TOP_K
    flat_e = e_idx.reshape(-1)
    order = jnp.argsort(flat_e).astype(jnp.int32)
    counts = jnp.sum((flat_e[:, None] == jnp.arange(ne, dtype=jnp.int32)[None, :]).astype(jnp.int32), axis=0)
    starts = jnp.cumsum(counts) - counts
    pcounts = (counts + bm - 1) // bm * bm
    pends = jnp.cumsum(pcounts)
    pstarts = pends - pcounts
    n_blocks = -(-n_assign // bm) + ne
    blk_start = jnp.arange(n_blocks, dtype=jnp.int32) * bm
    block_e = jnp.minimum(jnp.sum((pends[None, :] <= blk_start[:, None]).astype(jnp.int32), axis=1), ne - 1)
    n_used = (pends[ne - 1:] // bm).astype(jnp.int32)
    blk_off = blk_start - pstarts[block_e]
    n_valid = jnp.clip(counts[block_e] - blk_off, 0, bm).astype(jnp.int32)
    off = blk_off[:, None] + jnp.arange(bm, dtype=jnp.int32)[None, :]
    src = jnp.clip(starts[block_e][:, None] + off, 0, n_assign - 1)
    row_slot = jnp.where(off < counts[block_e][:, None], order[src.reshape(-1)].reshape(n_blocks, bm), 0)
    y = _expert_mlp(block_e, n_valid, n_used, row_slot.reshape(-1), h, w_gate, w_up, w_down, layer)
    return y, w


def _rope_tables(pos, rope):
    inv = 1.0 / (ROPE_THETA ** (jnp.arange(0, rope, 2, dtype=F32) / rope))
    ang = pos.astype(F32)[:, None] * inv[None, :]
    cos, sin = jnp.cos(ang), jnp.sin(ang)
    pad = jnp.zeros((pos.shape[0], LANES - rope), F32)
    return jnp.concatenate([cos, cos, pad], axis=1), jnp.concatenate([-sin, sin, pad], axis=1)


def kernel(x_prompt, x_sample, c_prompt, c_sample, state_ssm_re, state_ssm_im, cache_ckv, cache_krope, w_ada, b_ada, ln_g, ln_b, ssm_a_re, ssm_a_im, ssm_log_dt, ssm_b_re, ssm_b_im, ssm_c_re, ssm_c_im, ssm_d, ssm_w_glu, mla_w_in, mla_q_norm, mla_kv_norm, mla_w_uq, mla_w_uk, mla_w_uv, mla_w_o, w_router, router_bias, moe_w_gate, moe_w_up, moe_w_down):
    depth, d = w_ada.shape[0], w_ada.shape[1]
    alpha = (2 * depth) ** 0.25
    nh, nope, vd = mla_w_uk.shape[2], mla_w_uk.shape[3], mla_w_uv.shape[3]
    ql, kvl = mla_q_norm.shape[1], mla_kv_norm.shape[1]
    rope = mla_w_uq.shape[3] - nope
    assert nope == LANES and vd == LANES and rope <= LANES
    scale = (nope + rope) ** -0.5 * math.log2(math.e)
    n_groups, n_state = ssm_a_re.shape[1], ssm_a_re.shape[2]
    past = cache_ckv.shape[2]

    xs = [x_prompt, x_sample]
    bsz = [x.shape[0] for x in xs]
    seq = [x.shape[1] for x in xs]
    row0 = [0, bsz[0] * seq[0]]
    n_tokens = row0[1] + bsz[1] * seq[1]
    pos = [jnp.arange(seq[0]), past + jnp.arange(seq[1])]
    tabs = [_rope_tables(p, rope) for p in pos]

    mod = _ada_mod(jnp.concatenate([c_prompt, c_sample], axis=0), w_ada, b_ada)

    def mod_rows(i, grp, piece):
        b0 = 0 if grp == 0 else bsz[0]
        return mod[i, b0:b0 + bsz[grp], piece * d:(piece + 1) * d].reshape(bsz[grp], 1, d)

    assert depth % 2 == 0, "layers alternate S5 / MLA mixers, starting with S5"
    w_router_b = w_router.astype(BF16)
    half = rope // 2
    ssm_out = [[[], []], [[], []]]
    ckv_out, kr_out = [[], []], [[], []]

    h_in = [None, None]
    for i in range(depth):
        j = i // 2
        x1s, tokens = [], jnp.zeros((n_tokens, d), F32)
        if i % 2 == 0:
            bd, cd, ab = _s5_tables(ssm_a_re[j], ssm_a_im[j], ssm_log_dt[j], ssm_b_re[j], ssm_b_im[j],
                                    ssm_c_re[j], ssm_c_im[j])
            w_glu = ssm_w_glu[j].astype(BF16)
        else:
            w_in = mla_w_in[j]
            r0 = ql + kvl
            zc = jnp.zeros((d, LANES - rope), F32)
            w_in_ext = jnp.concatenate([w_in, zc, w_in[:, r0 + half:], w_in[:, r0:r0 + half], zc], axis=1).astype(BF16)
            wq = mla_w_uq[j]
            zq = jnp.zeros((ql, nh, LANES - rope), F32)
            wq_rope = jnp.concatenate([wq[..., nope:], zq], axis=-1)
            wq_rot = jnp.concatenate([wq[..., nope + half:], wq[..., nope:nope + half], zq], axis=-1)
            wq = jnp.concatenate([wq[..., :nope].reshape(ql, -1), wq_rope.reshape(ql, -1),
                                  wq_rot.reshape(ql, -1)], axis=1).astype(BF16)
            wkv = jnp.concatenate([mla_w_uk[j].reshape(kvl, -1), mla_w_uv[j].reshape(kvl, -1)],
                                  axis=1).astype(BF16)
            w_ukt = mla_w_uk[j].transpose(1, 2, 0).astype(BF16)
            w_uvh = mla_w_uv[j].transpose(1, 0, 2).astype(BF16)
            w_o = mla_w_o[j].astype(BF16)
        for grp in range(2):
            x = xs[grp]
            b, l = bsz[grp], seq[grp]
            sh1, sc1, g1, sh2, sc2 = (mod_rows(i, grp, p) for p in range(5))
            lng, lnb = ln_g[i, 0].reshape(1, d), ln_b[i, 0].reshape(1, d)
            if i % 2 == 0:
                nbg = b // SUBLANES
                if grp == 0:
                    h0r = jnp.zeros((nbg, SUBLANES, n_groups * n_state), F32)
                    h0i = h0r
                else:
                    h0r = state_ssm_re[j].reshape(nbg, SUBLANES, n_groups * n_state)
                    h0i = state_ssm_im[j].reshape(nbg, SUBLANES, n_groups * n_state)
                z, hr, hi = _s5_scan(x, sc1, sh1, bd, cd, ab, ssm_d[j].reshape(1, d), h0r, h0i)
                ssm_out[grp][0].append(hr.reshape(b, n_groups, n_state))
                ssm_out[grp][1].append(hi.reshape(b, n_groups, n_state))

                def a_spec(tm):
                    return pl.BlockSpec((None, tm, d), lambda bi, t: (bi, t, 0))

                x1, tokens = _mix_ln(z, a_spec, w_glu, x, g1, lng, lnb, sc2, sh2, tokens, row0[grp],
                                     glu=True, alpha=alpha)
            else:
                cos_t, sin_t = tabs[grp]
                cq, ckv, kr, krp = _mla_in(h_in[grp], w_in_ext, mla_q_norm[j].reshape(1, ql),
                                           mla_kv_norm[j].reshape(1, kvl), cos_t, sin_t, ql=ql, kvl=kvl, rope=rope)
                q = _mla_q(cq, wq, cos_t, sin_t, nh=nh, scale=scale)
                if grp == 0:
                    k, v = _mla_kv(ckv, krp, wkv, nh=nh)
                    o = _prompt_attn(q, k, v, vd=vd)
                else:
                    q_lat = _absorb_q(q, w_ukt, nope=nope)
                    o_lat = _cached_attn(q_lat, q, cache_ckv[j], cache_krope[j], ckv, kr, nope=nope, rope=rope)
                    o = _absorb_v(o_lat, w_uvh)
                ckv_out[grp].append(ckv)
                kr_out[grp].append(kr)

                def a_spec(tm):
                    return pl.BlockSpec((None, tm, nh * vd), lambda bi, t: (bi, t, 0))

                x1, tokens = _mix_ln(o, a_spec, w_o, x, g1, lng, lnb, sc2, sh2, tokens, row0[grp],
                                     glu=False, alpha=alpha)
            x1s.append(x1)
        y_all, wt_all = _moe(tokens, w_router_b, router_bias, moe_w_gate, moe_w_up, moe_w_down, i)
        for grp in range(2):
            g2 = mod_rows(i, grp, 5)
            next_mod = (mod_rows(i + 1, grp, 1), mod_rows(i + 1, grp, 0)) if i % 2 == 0 else None
            xs[grp], h_in[grp] = _ffn_ln(y_all, wt_all, row0[grp], x1s[grp], g2, ln_g[i, 1].reshape(1, d),
                                         ln_b[i, 1].reshape(1, d), next_mod, alpha=alpha)

    return (xs[0], xs[1],
            jnp.stack(ssm_out[0][0]), jnp.stack(ssm_out[0][1]), jnp.stack(ckv_out[0]), jnp.stack(kr_out[0]),
            jnp.stack(ssm_out[1][0]), jnp.stack(ssm_out[1][1]), jnp.stack(ckv_out[1]), jnp.stack(kr_out[1]))
```

```python
import functools
import math

import jax
import jax.numpy as jnp
from jax import lax
from jax.experimental import pallas as pl
from jax.experimental.pallas import tpu as pltpu

F32 = jnp.float32
BF16 = jnp.bfloat16

CHUNK = 64
SSM_GROUP_WIDTH = 16
N_EXPERT_GROUPS = 8
TOP_K = 2
ROPE_THETA = 10000.0
LN_EPS = 1e-5
RMS_EPS = 1e-6
NEG_INF = -1e30

V7X_VMEM_BYTES = 64 * 1024 * 1024
SUBLANES = 8
LANES = 128
MXU_DIM = 256

VMEM_LIMIT = 56 * 1024 * 1024
S5_CHANNELS = MXU_DIM
S5_TIME = 256
ROW_TILE = 256
ATTN_BLOCK = 512
ATTN_HEADS = 4
MOE_ROWS = 256
CACHE_KEYS = 1024


def _cparams(sem):
    return pltpu.CompilerParams(dimension_semantics=sem, vmem_limit_bytes=VMEM_LIMIT)


def _bdot(a, b):
    return jnp.dot(a, b, preferred_element_type=F32)


def _ada_kernel(c_ref, w_ref, b_ref, o_ref):
    c = c_ref[...]
    cs = (c * jax.nn.sigmoid(c)).astype(BF16)
    o_ref[...] = _bdot(cs, w_ref[...].astype(BF16)) + b_ref[...]


def _ada_mod(c_all, w_ada, b_ada):
    depth, d, n = w_ada.shape
    bc = c_all.shape[0]
    tn = 1024
    return pl.pallas_call(
        _ada_kernel,
        grid=(depth, n // tn),
        in_specs=[pl.BlockSpec((bc, d), lambda i, j: (0, 0)),
                  pl.BlockSpec((None, d, tn), lambda i, j: (i, 0, j)),
                  pl.BlockSpec((None, 1, tn), lambda i, j: (i, 0, j))],
        out_specs=pl.BlockSpec((None, bc, tn), lambda i, j: (i, 0, j)),
        out_shape=jax.ShapeDtypeStruct((depth, bc, n), F32),
        compiler_params=_cparams(("parallel", "parallel")),
        name="ada_mod",
    )(c_all, w_ada, b_ada.reshape(depth, 1, n))


def _s5_kernel(x_ref, sc_ref, sh_ref, bd_ref, cd_ref, ab_ref, d_ref, h0r_ref, h0i_ref,
               z_ref, hr_ref, hi_ref, u_ref, xs_ref, hs_ref, h_ref, *, tt, sb):
    t = pl.program_id(2)

    @pl.when(t == 0)
    def _():
        h_ref[0] = h0r_ref[...]
        h_ref[1] = h0i_ref[...]

    lane_tiles = range(u_ref.shape[0])
    for b in range(SUBLANES):
        ub = x_ref[b] * (1.0 + sc_ref[b]) + sh_ref[b]
        for c in lane_tiles:
            u_ref[c, pl.ds(b, tt, stride=SUBLANES), :] = ub[:, c * LANES:(c + 1) * LANES]
    u = jnp.concatenate([u_ref[c] for c in lane_tiles], axis=1)
    xs_ref[...] = _bdot(u.astype(BF16), bd_ref[...])
    a_re = jnp.broadcast_to(ab_ref[0:1, :], (SUBLANES, sb))
    a_im = jnp.broadcast_to(ab_ref[1:2, :], (SUBLANES, sb))

    pair = 2 * SUBLANES

    def step(i, carry):
        h_re, h_im = carry
        r0 = pl.multiple_of(i * pair, pair)
        res, ims = [], []
        for s in range(2):
            x_re = xs_ref[pl.ds(r0 + s * SUBLANES, SUBLANES), 0:sb]
            x_im = xs_ref[pl.ds(r0 + s * SUBLANES, SUBLANES), sb:2 * sb]
            h_re, h_im = a_re * h_re - a_im * h_im + x_re, a_re * h_im + a_im * h_re + x_im
            res.append(h_re)
            ims.append(h_im)
        hs_ref[pl.ds(r0, pair), 0:sb] = jnp.concatenate(res, axis=0).astype(BF16)
        hs_ref[pl.ds(r0, pair), sb:2 * sb] = jnp.concatenate(ims, axis=0).astype(BF16)
        return h_re, h_im

    h_re, h_im = lax.fori_loop(0, tt // 2, step, (h_ref[0], h_ref[1]))
    h_ref[0] = h_re
    h_ref[1] = h_im
    y = _bdot(hs_ref[...], cd_ref[...]) + d_ref[...] * u
    zf = jax.nn.gelu(y)
    for c in lane_tiles:
        u_ref[c] = zf[:, c * LANES:(c + 1) * LANES]
    for b in range(SUBLANES):
        for c in lane_tiles:
            z_ref[b, :, c * LANES:(c + 1) * LANES] = (
                u_ref[c, pl.ds(b, tt, stride=SUBLANES), :].astype(z_ref.dtype))

    @pl.when(t == pl.num_programs(2) - 1)
    def _():
        hr_ref[...] = h_re
        hi_ref[...] = h_im


def _s5_scan(x, sc, sh, bd, cd, ab, d_skip, h0_re, h0_im):
    b, l, d = x.shape
    nbg = b // SUBLANES
    nj, cb, sb2 = bd.shape
    sb = sb2 // 2
    tt = min(l, S5_TIME)
    kern = functools.partial(_s5_kernel, tt=tt, sb=sb)
    tile = pl.BlockSpec((SUBLANES, tt, cb), lambda g, j, t: (g, t, j))
    row = pl.BlockSpec((SUBLANES, 1, cb), lambda g, j, t: (g, 0, j))
    state_spec = pl.BlockSpec((None, SUBLANES, sb), lambda g, j, t: (g, 0, j))
    state_shape = jax.ShapeDtypeStruct((nbg, SUBLANES, nj * sb), F32)
    return pl.pallas_call(
        kern,
        grid=(nbg, nj, l // tt),
        in_specs=[tile, row, row,
                  pl.BlockSpec((None, cb, sb2), lambda g, j, t: (j, 0, 0)),
                  pl.BlockSpec((None, sb2, cb), lambda g, j, t: (j, 0, 0)),
                  pl.BlockSpec((None, 2, sb), lambda g, j, t: (j, 0, 0)),
                  pl.BlockSpec((1, cb), lambda g, j, t: (0, j)),
                  state_spec, state_spec],
        out_specs=[tile, state_spec, state_spec],
        out_shape=[jax.ShapeDtypeStruct((b, l, d), BF16), state_shape, state_shape],
        scratch_shapes=[pltpu.VMEM((cb // LANES, tt * SUBLANES, LANES), F32), pltpu.VMEM((tt * SUBLANES, sb2), F32),
                        pltpu.VMEM((tt * SUBLANES, sb2), BF16), pltpu.VMEM((2, SUBLANES, sb), F32)],
        compiler_params=_cparams(("parallel", "parallel", "arbitrary")),
        name="s5_scan",
    )(x, sc, sh, bd, cd, ab, d_skip, h0_re, h0_im)


def _s5_tables(a_re, a_im, log_dt, b_re, b_im, c_re, c_im):
    g, p = a_re.shape
    w = b_re.shape[-1]
    gl = S5_CHANNELS // w
    nj = g // gl
    dt = jnp.exp(log_dt)[:, None]
    mag = jnp.exp(a_re * dt)
    abar_re, abar_im = mag * jnp.cos(a_im * dt), mag * jnp.sin(a_im * dt)
    den = a_re * a_re + a_im * a_im
    n_re, n_im = abar_re - 1.0, abar_im
    f_re = (n_re * a_re + n_im * a_im) / den
    f_im = (n_im * a_re - n_re * a_im) / den
    bb_re = f_re[..., None] * b_re - f_im[..., None] * b_im
    bb_im = f_re[..., None] * b_im + f_im[..., None] * b_re
    eye = jnp.eye(gl, dtype=F32)
    bb = jnp.stack([bb_re, bb_im]).reshape(2, nj, gl, p, w)
    bd = jnp.einsum('sjgpw,gh->jgwshp', bb, eye).reshape(nj, gl * w, 2 * gl * p)
    cc = jnp.stack([c_re, -c_im]).reshape(2, nj, gl, w, p)
    cd = jnp.einsum('sjgwp,gh->jsgphw', cc, eye).reshape(nj, 2 * gl * p, gl * w)
    ab = jnp.stack([abar_re.reshape(nj, gl * p), abar_im.reshape(nj, gl * p)], axis=1)
    return bd.astype(BF16), cd.astype(BF16), ab


def _layer_norm(y, g, b):
    mu = jnp.mean(y, axis=-1, keepdims=True)
    yc = y - mu
    var = jnp.mean(yc * yc, axis=-1, keepdims=True)
    return yc * lax.rsqrt(var + LN_EPS) * g + b


def _mix_ln_kernel(a_ref, w_ref, x_ref, gate_ref, lng_ref, lnb_ref, sc_ref, sh_ref, tokens_hbm,
                   x1_ref, h_ref, *, glu, alpha):
    del tokens_hbm
    acc = _bdot(a_ref[...], w_ref[...])
    if glu:
        d = acc.shape[-1] // 2
        m = acc[:, :d] * jax.nn.sigmoid(acc[:, d:])
    else:
        m = acc
    x1 = _layer_norm(alpha * x_ref[...] + gate_ref[...] * m, lng_ref[...], lnb_ref[...])
    x1_ref[...] = x1
    h_ref[...] = x1 * (1.0 + sc_ref[...]) + sh_ref[...]


def _mix_ln(a, a_spec, w, x, gate, ln_g, ln_b, sc, sh, tokens, row0, *, glu, alpha):
    b, l, d = x.shape
    tm = min(l, ROW_TILE)
    nt = l // tm
    blk0 = row0 // tm
    k, n = w.shape
    row = pl.BlockSpec((None, 1, d), lambda i, t: (i, 0, 0))
    vec = pl.BlockSpec((1, d), lambda i, t: (0, 0))
    tile = pl.BlockSpec((None, tm, d), lambda i, t: (i, t, 0))
    return pl.pallas_call(
        functools.partial(_mix_ln_kernel, glu=glu, alpha=alpha),
        grid=(b, nt),
        in_specs=[a_spec(tm), pl.BlockSpec((k, n), lambda i, t: (0, 0), pipeline_mode=pl.Buffered(1)),
                  tile, row, vec, vec, row, row, pl.BlockSpec(memory_space=pl.ANY)],
        out_specs=[tile, pl.BlockSpec((tm, d), lambda i, t: (blk0 + i * nt + t, 0))],
        out_shape=[jax.ShapeDtypeStruct((b, l, d), F32), jax.ShapeDtypeStruct(tokens.shape, F32)],
        input_output_aliases={8: 1},
        compiler_params=_cparams(("parallel", "parallel")),
        name="mix_ln_glu" if glu else "mix_ln_proj",
    )(a, w, x, gate, ln_g, ln_b, sc, sh, tokens)


def _ffn_ln_kernel(*refs, alpha):
    y_refs, (wt_ref, x_ref, gate_ref, lng_ref, lnb_ref), rest = refs[:TOP_K], refs[TOP_K:TOP_K + 5], refs[TOP_K + 5:]
    f = wt_ref[:, 0:1] * y_refs[0][...]
    for k in range(1, TOP_K):
        f = f + wt_ref[:, k:k + 1] * y_refs[k][...]
    x2 = _layer_norm(alpha * x_ref[...] + gate_ref[...] * f, lng_ref[...], lnb_ref[...])
    if len(rest) == 1:
        rest[0][...] = x2
    else:
        sc_ref, sh_ref, x2_ref, h_ref = rest
        x2_ref[...] = x2
        h_ref[...] = (x2 * (1.0 + sc_ref[...]) + sh_ref[...]).astype(h_ref.dtype)


def _ffn_ln(y_all, wt_all, row0, x, gate, ln_g, ln_b, next_mod, *, alpha):
    b, l, d = x.shape
    tm = min(l, ROW_TILE)
    nt = l // tm
    blk0 = row0 // tm
    kblk = wt_all.shape[0] // tm
    row = pl.BlockSpec((None, 1, d), lambda i, t: (i, 0, 0))
    vec = pl.BlockSpec((1, d), lambda i, t: (0, 0))
    tile = pl.BlockSpec((None, tm, d), lambda i, t: (i, t, 0))
    in_specs = [pl.BlockSpec((tm, d), functools.partial(lambda k, i, t: (k * kblk + blk0 + i * nt + t, 0), k))
                for k in range(TOP_K)]
    in_specs += [pl.BlockSpec((tm, TOP_K), lambda i, t: (blk0 + i * nt + t, 0)), tile, row, vec, vec]
    x2_shape = jax.ShapeDtypeStruct((b, l, d), F32)
    if next_mod is None:
        args, out_specs, out_shape = (), tile, x2_shape
    else:
        args, in_specs = tuple(next_mod), in_specs + [row, row]
        out_specs, out_shape = [tile, tile], [x2_shape, jax.ShapeDtypeStruct((b, l, d), BF16)]
    out = pl.pallas_call(
        functools.partial(_ffn_ln_kernel, alpha=alpha),
        grid=(b, nt),
        in_specs=in_specs,
        out_specs=out_specs,
        out_shape=out_shape,
        compiler_params=_cparams(("parallel", "parallel")),
        name="ffn_ln",
    )(*([y_all] * TOP_K), wt_all, x, gate, ln_g, ln_b, *args)
    return (out, None) if next_mod is None else out


def _rms_norm(x, g):
    ms = jnp.mean(x * x, axis=-1, keepdims=True)
    return x * lax.rsqrt(ms + RMS_EPS) * g


def _mla_in_kernel(h_ref, w_ref, qn_ref, kvn_ref, cos_ref, sin_ref,
                   cq_ref, ckv_ref, kr_ref, krp_ref, *, ql, kvl, rope):
    acc = _bdot(h_ref[...], w_ref[...])
    cq_ref[...] = _rms_norm(acc[:, :ql], qn_ref[...]).astype(cq_ref.dtype)
    ckv_ref[...] = _rms_norm(acc[:, ql:ql + kvl], kvn_ref[...])
    o = ql + kvl
    kr = acc[:, o:o + LANES] * cos_ref[...] + acc[:, o + LANES:o + 2 * LANES] * sin_ref[...]
    kr_ref[...] = kr[:, :rope]
    krp_ref[...] = kr.astype(krp_ref.dtype)


def _mla_in(h, w_in_ext, q_norm, kv_norm, cos_t, sin_t, *, ql, kvl, rope):
    b, l, d = h.shape
    tm = min(l, ROW_TILE)
    n = w_in_ext.shape[1]
    kern = functools.partial(_mla_in_kernel, ql=ql, kvl=kvl, rope=rope)
    tab = pl.BlockSpec((tm, LANES), lambda i, t: (t, 0))

    def out(nf):
        return pl.BlockSpec((None, tm, nf), lambda i, t: (i, t, 0))

    return pl.pallas_call(
        kern,
        grid=(b, l // tm),
        in_specs=[pl.BlockSpec((None, tm, d), lambda i, t: (i, t, 0)),
                  pl.BlockSpec((d, n), lambda i, t: (0, 0)),
                  pl.BlockSpec((1, ql), lambda i, t: (0, 0)),
                  pl.BlockSpec((1, kvl), lambda i, t: (0, 0)), tab, tab],
        out_specs=[out(ql), out(kvl), out(rope), out(LANES)],
        out_shape=[jax.ShapeDtypeStruct((b, l, ql), BF16), jax.ShapeDtypeStruct((b, l, kvl), F32),
                   jax.ShapeDtypeStruct((b, l, rope), F32), jax.ShapeDtypeStruct((b, l, LANES), BF16)],
        compiler_params=_cparams(("parallel", "parallel")),
        name="mla_in",
    )(h, w_in_ext, q_norm, kv_norm, cos_t, sin_t)


def _mla_q_kernel(cq_ref, w_ref, cos_ref, sin_ref, q_ref, *, nh, scale):
    acc = _bdot(cq_ref[...], w_ref[...])
    cos = cos_ref[...] * scale
    sin = sin_ref[...] * scale
    for h in range(nh):
        q_ref[h, :, :LANES] = (acc[:, h * LANES:(h + 1) * LANES] * scale).astype(q_ref.dtype)
        r0, p0 = (nh + h) * LANES, (2 * nh + h) * LANES
        q_ref[h, :, LANES:] = (acc[:, r0:r0 + LANES] * cos + acc[:, p0:p0 + LANES] * sin).astype(q_ref.dtype)


def _mla_q(cq, wq, cos_t, sin_t, *, nh, scale):
    b, l, ql = cq.shape
    n = wq.shape[1]
    tm = min(l, ROW_TILE)
    tab = pl.BlockSpec((tm, LANES), lambda i, t: (t, 0))
    return pl.pallas_call(
        functools.partial(_mla_q_kernel, nh=nh, scale=scale),
        grid=(b, l // tm),
        in_specs=[pl.BlockSpec((None, tm, ql), lambda i, t: (i, t, 0)),
                  pl.BlockSpec((ql, n), lambda i, t: (0, 0)), tab, tab],
        out_specs=pl.BlockSpec((None, nh, tm, 2 * LANES), lambda i, t: (i, 0, t, 0)),
        out_shape=jax.ShapeDtypeStruct((b, nh, l, 2 * LANES), BF16),
        compiler_params=_cparams(("parallel", "parallel")),
        name="mla_q",
    )(cq, wq, cos_t, sin_t)


def _mla_kv_kernel(ckv_ref, w_ref, k_ref, v_ref, *, nh):
    acc = _bdot(ckv_ref[...].astype(BF16), w_ref[...])
    for h in range(nh):
        k_ref[h] = acc[:, h * LANES:(h + 1) * LANES].astype(k_ref.dtype)
        v_ref[h] = acc[:, (nh + h) * LANES:(nh + h + 1) * LANES].astype(v_ref.dtype)


def _mla_kv(ckv, wkv, *, nh):
    b, l, kvl = ckv.shape
    n = wkv.shape[1]
    tm = min(l, ROW_TILE)
    head_tile = pl.BlockSpec((None, nh, tm, LANES), lambda i, t: (i, 0, t, 0))
    head_shape = jax.ShapeDtypeStruct((b, nh, l, LANES), BF16)
    return pl.pallas_call(
        functools.partial(_mla_kv_kernel, nh=nh),
        grid=(b, l // tm),
        in_specs=[pl.BlockSpec((None, tm, kvl), lambda i, t: (i, t, 0)),
                  pl.BlockSpec((kvl, n), lambda i, t: (0, 0))],
        out_specs=[head_tile, head_tile],
        out_shape=[head_shape, head_shape],
        compiler_params=_cparams(("parallel", "parallel")),
        name="mla_kv",
    )(ckv, wkv)


def _qk(q, k):
    return lax.dot_general(q, k, (((1,), (1,)), ((), ())), preferred_element_type=F32)


def _softmax_step(carry, s, v):
    m, l, acc = carry
    m_new = jnp.maximum(m, jnp.max(s, axis=-1, keepdims=True))
    p = jnp.exp2(s - m_new)
    alpha = jnp.exp2(m - m_new)
    l = alpha * l + jnp.sum(p, axis=-1, keepdims=True)
    acc = alpha * acc + _bdot(p.astype(BF16), v)
    return m_new, l, acc


def _softmax_step_ones(carry, s, v):
    m, acc = carry
    m_new = jnp.maximum(m, jnp.max(s, axis=-1, keepdims=True))
    p = jnp.exp2(s - m_new)
    acc = jnp.exp2(m - m_new) * acc + _bdot(p.astype(BF16), v)
    return m_new, acc


def _prompt_attn_kernel(q_ref, kn_ref, kr_ref, v_ref, o_ref, *, blk, vd):
    i = pl.program_id(2)
    heads = range(q_ref.shape[0])
    qs = [q_ref[h] for h in heads]
    lane = lax.broadcasted_iota(jnp.int32, (blk, LANES), 1)
    ones_col = jnp.where(lane == 0, 1.0, 0.0).astype(v_ref.dtype)

    def keys(h, r):
        return jnp.concatenate([kn_ref[h, r, :], kr_ref[r, :]], axis=1)

    def values(h, r):
        return jnp.concatenate([v_ref[h, r, :], ones_col], axis=1)

    def body(j, carry):
        r = pl.ds(pl.multiple_of(j * blk, blk), blk)
        return tuple(_softmax_step_ones(carry[h], _qk(qs[h], keys(h, r)), values(h, r)) for h in heads)

    init = (jnp.full((blk, 1), NEG_INF, F32), jnp.zeros((blk, 2 * LANES), F32))
    carry = lax.fori_loop(0, i, body, tuple(init for _ in heads))
    r = pl.ds(pl.multiple_of(i * blk, blk), blk)
    q_chunk = lax.broadcasted_iota(jnp.int32, (blk, blk), 0) // CHUNK
    k_chunk = lax.broadcasted_iota(jnp.int32, (blk, blk), 1) // CHUNK
    visible = k_chunk <= q_chunk
    for h in heads:
        s = jnp.where(visible, _qk(qs[h], keys(h, r)), NEG_INF)
        _, acc = _softmax_step_ones(carry[h], s, values(h, r))
        o_ref[:, h * vd:(h + 1) * vd] = (acc[:, :vd] / acc[:, vd:vd + 1]).astype(o_ref.dtype)


def _prompt_attn(q, kn, kr, v):
    b, nh, l, dk = q.shape
    vd = v.shape[-1]
    blk = min(l, ATTN_BLOCK)
    hp = ATTN_HEADS
    per_head = pl.BlockSpec((None, hp, l, LANES), lambda i, h, t: (i, h, 0, 0))
    return pl.pallas_call(
        functools.partial(_prompt_attn_kernel, blk=blk, vd=vd),
        grid=(b, nh // hp, l // blk),
        in_specs=[pl.BlockSpec((None, hp, blk, dk), lambda i, h, t: (i, h, t, 0)), per_head,
                  pl.BlockSpec((None, l, LANES), lambda i, h, t: (i, 0, 0)), per_head],
        out_specs=pl.BlockSpec((None, blk, hp * vd), lambda i, h, t: (i, t, h)),
        out_shape=jax.ShapeDtypeStruct((b, l, nh * vd), BF16),
        compiler_params=_cparams(("parallel", "parallel", "arbitrary")),
        name="prompt_attn",
    )(q, kn, kr, v)


def _absorb_q_kernel(q_ref, w_ref, o_ref, *, nope):
    b, lq, dk = q_ref.shape
    q = q_ref[...].reshape(b * lq, dk)[:, :nope]
    o_ref[...] = _bdot(q, w_ref[...]).reshape(b, lq, -1).astype(o_ref.dtype)


def _absorb_q(q, w_ukt, *, nope):
    b, nh, lq, dk = q.shape
    kvl = w_ukt.shape[-1]
    return pl.pallas_call(
        functools.partial(_absorb_q_kernel, nope=nope),
        grid=(nh,),
        in_specs=[pl.BlockSpec((b, None, lq, dk), lambda h: (0, h, 0, 0)),
                  pl.BlockSpec((None, nope, kvl), lambda h: (h, 0, 0))],
        out_specs=pl.BlockSpec((b, None, lq, kvl), lambda h: (0, h, 0, 0)),
        out_shape=jax.ShapeDtypeStruct((b, nh, lq, kvl), BF16),
        compiler_params=_cparams(("parallel",)),
        name="absorb_q",
    )(q, w_ukt)


def _cached_attn_kernel(ql_ref, q_ref, cc_ref, ck_ref, nc_ref, nk_ref, o_ref, *, nope, rope, past, kb):
    nh, lq, kvl = ql_ref.shape
    rows = nh * lq
    ql = ql_ref[...].reshape(rows, kvl)
    qr = q_ref[...].reshape(rows, -1)[:, nope:nope + rope]
    q_pos = past + lax.broadcasted_iota(jnp.int32, (rows, 1), 0) % lq
    carry = (jnp.full((rows, 1), NEG_INF, F32), jnp.zeros((rows, 1), F32), jnp.zeros((rows, kvl), F32))

    def scores(c, kr, k0):
        s = _qk(ql, c) + _qk(qr, kr)
        k_pos = k0 + lax.broadcasted_iota(jnp.int32, s.shape, 1)
        return jnp.where(k_pos // CHUNK <= q_pos // CHUNK, s, NEG_INF)

    for j in range(past // kb):
        c = cc_ref[j * kb:(j + 1) * kb, :].astype(BF16)
        kr = ck_ref[j * kb:(j + 1) * kb, :].astype(BF16)
        carry = _softmax_step(carry, scores(c, kr, j * kb), c)
    c = nc_ref[...].astype(BF16)
    kr = nk_ref[...].astype(BF16)
    _, l, acc = _softmax_step(carry, scores(c, kr, past), c)
    o_ref[...] = (acc / l).reshape(nh, lq, kvl).astype(o_ref.dtype)


def _cached_attn(q_lat, q, cache_ckv, cache_kr, new_ckv, new_kr, *, nope, rope):
    b, nh, lq, kvl = q_lat.shape
    past = cache_ckv.shape[1]
    kb = min(past, CACHE_KEYS)
    dk = q.shape[-1]
    kern = functools.partial(_cached_attn_kernel, nope=nope, rope=rope, past=past, kb=kb)
    return pl.pallas_call(
        kern,
        grid=(b,),
        in_specs=[pl.BlockSpec((None, nh, lq, kvl), lambda i: (i, 0, 0, 0)),
                  pl.BlockSpec((None, nh, lq, dk), lambda i: (i, 0, 0, 0)),
                  pl.BlockSpec((None, past, kvl), lambda i: (i, 0, 0)),
                  pl.BlockSpec((None, past, rope), lambda i: (i, 0, 0)),
                  pl.BlockSpec((None, lq, kvl), lambda i: (i, 0, 0)),
                  pl.BlockSpec((None, lq, rope), lambda i: (i, 0, 0))],
        out_specs=pl.BlockSpec((None, nh, lq, kvl), lambda i: (i, 0, 0, 0)),
        out_shape=jax.ShapeDtypeStruct((b, nh, lq, kvl), BF16),
        compiler_params=_cparams(("parallel",)),
        name="cached_attn",
    )(q_lat, q, cache_ckv, cache_kr, new_ckv, new_kr)


def _absorb_v_kernel(o_ref, w_ref, out_ref):
    b, lq, kvl = o_ref.shape
    out_ref[...] = _bdot(o_ref[...].reshape(b * lq, kvl), w_ref[...]).reshape(b, lq, -1).astype(out_ref.dtype)


def _absorb_v(o_lat, w_uv):
    b, nh, lq, kvl = o_lat.shape
    vd = w_uv.shape[-1]
    return pl.pallas_call(
        _absorb_v_kernel,
        grid=(nh,),
        in_specs=[pl.BlockSpec((b, None, lq, kvl), lambda h: (0, h, 0, 0)),
                  pl.BlockSpec((None, kvl, vd), lambda h: (h, 0, 0))],
        out_specs=pl.BlockSpec((b, lq, vd), lambda h: (0, 0, h)),
        out_shape=jax.ShapeDtypeStruct((b, lq, nh * vd), BF16),
        compiler_params=_cparams(("parallel",)),
        name="absorb_v",
    )(o_lat, w_uv)


def _router_kernel(x_ref, w_ref, o_ref):
    o_ref[...] = _bdot(x_ref[...].astype(BF16), w_ref[...])


def _router_logits(x, w):
    t, d = x.shape
    e = w.shape[1]
    tm = math.gcd(t, 1024)
    return pl.pallas_call(
        _router_kernel,
        grid=(t // tm,),
        in_specs=[pl.BlockSpec((tm, d), lambda i: (i, 0)), pl.BlockSpec((d, e), lambda i: (0, 0))],
        out_specs=pl.BlockSpec((tm, e), lambda i: (i, 0)),
        out_shape=jax.ShapeDtypeStruct((t, e), F32),
        compiler_params=_cparams(("parallel",)),
        name="router_logits",
    )(x, w)


def _expert_kernel(be_ref, nv_ref, nb_ref, slot_ref, x_hbm, wg_ref, wu_ref, wd_ref, y_hbm,
                   xbuf, obuf, wgb_ref, wub_ref, wdb_ref, gsem, ssem, *, bm):
    i = pl.program_id(0)
    n_used = nb_ref[0]
    cur = lax.rem(i, 2)
    n_tok = x_hbm.shape[0]

    def token_of(slot):
        if TOP_K & (TOP_K - 1) == 0:
            return lax.shift_right_logical(slot, TOP_K.bit_length() - 1)
        return lax.div(slot, TOP_K)

    def start_gather(blk, buf):
        base = blk * bm
        for r in range(bm):
            tok = token_of(slot_ref[base + r])
            pltpu.make_async_copy(x_hbm.at[pl.ds(tok, 1), :], xbuf.at[buf, pl.ds(r, 1), :], gsem.at[buf]).start()

    def scatter_row(base, buf, r):
        slot = slot_ref[base + r]
        tok = token_of(slot)
        dst = (slot - tok * TOP_K) * n_tok + tok
        pltpu.make_async_copy(obuf.at[buf, pl.ds(r, 1), :], y_hbm.at[pl.ds(dst, 1), :], ssem.at[buf]).start()

    def start_scatter(blk, buf):
        base = blk * bm
        nv = nv_ref[blk]

        @pl.when(nv == bm)
        def _():
            for r in range(bm):
                scatter_row(base, buf, r)

        @pl.when(nv < bm)
        def _():
            def group(g, carry):
                r0 = pl.multiple_of(g * SUBLANES, SUBLANES)
                for j in range(SUBLANES):
                    scatter_row(base, buf, r0 + j)
                return carry
            full = lax.shift_right_logical(nv, SUBLANES.bit_length() - 1)
            lax.fori_loop(0, full, group, 0)

            def single(r, carry):
                scatter_row(base, buf, r)
                return carry
            lax.fori_loop(full * SUBLANES, nv, single, 0)

    def wait_scatter(blk, buf):
        n = nv_ref[blk]
        piece = bm
        while piece >= 1:
            @pl.when((n & piece) != 0)
            def _(piece=piece):
                rows, times = (piece, 1) if piece >= SUBLANES else (1, piece)
                for _ in range(times):
                    pltpu.make_async_copy(obuf.at[buf, pl.ds(0, rows), :], y_hbm.at[pl.ds(0, rows), :],
                                          ssem.at[buf]).wait()
            piece //= 2

    @pl.when(i == 0)
    def _():
        start_gather(0, 0)

    def wait_gather(buf):
        pltpu.make_async_copy(x_hbm.at[pl.ds(0, bm), :], xbuf.at[buf], gsem.at[buf]).wait()

    @pl.when(i < n_used)
    def _():
        wait_gather(cur)

        @pl.when(jnp.logical_or(i == 0, be_ref[i] != be_ref[jnp.maximum(i - 1, 0)]))
        def _():
            wgb_ref[...] = wg_ref[...].astype(BF16)
            wub_ref[...] = wu_ref[...].astype(BF16)
            wdb_ref[...] = wd_ref[...].astype(BF16)

        @pl.when(i >= 2)
        def _():
            wait_scatter(i - 2, cur)

        start_gather(jnp.minimum(i + 1, n_used - 1), 1 - cur)
        x = xbuf[cur].astype(BF16)
        g = _bdot(x, wgb_ref[...])
        h = (g * jax.nn.sigmoid(g)) * _bdot(x, wub_ref[...])
        obuf[cur] = _bdot(h.astype(BF16), wdb_ref[...])
        start_scatter(i, cur)

    @pl.when(i == pl.num_programs(0) - 1)
    def _():
        wait_gather(lax.rem(n_used, 2))

        @pl.when(n_used >= 2)
        def _():
            wait_scatter(n_used - 2, lax.rem(n_used, 2))
        wait_scatter(n_used - 1, lax.rem(n_used - 1, 2))


def _expert_mlp(block_e, n_valid, n_used, row_slot, x, w_gate, w_up, w_down, layer):
    t, d = x.shape
    ff = w_gate.shape[-1]
    bm = MOE_ROWS
    n_blocks = block_e.shape[0]
    grid_spec = pltpu.PrefetchScalarGridSpec(
        num_scalar_prefetch=4,
        grid=(n_blocks,),
        in_specs=[pl.BlockSpec(memory_space=pl.ANY),
                  pl.BlockSpec((None, None, d, ff), lambda i, be, nv, nb, sl: (layer, be[i], 0, 0)),
                  pl.BlockSpec((None, None, d, ff), lambda i, be, nv, nb, sl: (layer, be[i], 0, 0)),
                  pl.BlockSpec((None, None, ff, d), lambda i, be, nv, nb, sl: (layer, be[i], 0, 0))],
        out_specs=pl.BlockSpec(memory_space=pl.ANY),
        scratch_shapes=[pltpu.VMEM((2, bm, d), F32), pltpu.VMEM((2, bm, d), F32),
                        pltpu.VMEM((d, ff), BF16), pltpu.VMEM((d, ff), BF16), pltpu.VMEM((ff, d), BF16),
                        pltpu.SemaphoreType.DMA((2,)), pltpu.SemaphoreType.DMA((2,))],
    )
    return pl.pallas_call(
        functools.partial(_expert_kernel, bm=bm),
        grid_spec=grid_spec,
        out_shape=jax.ShapeDtypeStruct((TOP_K * t, d), F32),
        compiler_params=_cparams(("arbitrary",)),
        name="expert_mlp",
    )(block_e, n_valid, n_used, row_slot, x, w_gate, w_up, w_down)


def _first_argmax(x):
    return jnp.argmax(x, axis=-1).astype(jnp.int32)


def _moe(h, w_router, router_bias, w_gate, w_up, w_down, layer):
    t, d = h.shape
    ne = w_router.shape[1]
    ng = N_EXPERT_GROUPS
    eg = ne // ng
    scores = jax.nn.sigmoid(_router_logits(h, w_router))
    sel = scores + router_bias
    grp = sel.reshape(t, ng, eg)
    lane = jnp.arange(eg, dtype=jnp.int32)
    i1 = _first_argmax(grp)
    m1 = jnp.max(grp, axis=-1)
    m2 = jnp.max(jnp.where(lane == i1[..., None], -jnp.inf, grp), axis=-1)
    g_idx = _first_argmax(m1 + m2)
    sel_g = jnp.take_along_axis(grp, g_idx[:, None, None], axis=1)[:, 0]
    sc_g = jnp.take_along_axis(scores.reshape(t, ng, eg), g_idx[:, None, None], axis=1)[:, 0]
    e1 = _first_argmax(sel_g)
    e2 = _first_argmax(jnp.where(lane == e1[:, None], -jnp.inf, sel_g))
    w1 = jnp.take_along_axis(sc_g, e1[:, None], axis=1)[:, 0]
    w2 = jnp.take_along_axis(sc_g, e2[:, None], axis=1)[:, 0]
    wsum = w1 + w2
    e_idx = jnp.stack([g_idx * eg + e1, g_idx * eg + e2], axis=1)
    w = jnp.stack([w1 / wsum, w2 / wsum], axis=1)

    bm = MOE_ROWS
    n_assign = t * TOP_K
    flat_e = e_idx.reshape(-1)
    order = jnp.argsort(flat_e).astype(jnp.int32)
    counts = jnp.sum((flat_e[:, None] == jnp.arange(ne, dtype=jnp.int32)[None, :]).astype(jnp.int32), axis=0)
    starts = jnp.cumsum(counts) - counts
    pcounts = (counts + bm - 1) // bm * bm
    pends = jnp.cumsum(pcounts)
    pstarts = pends - pcounts
    n_blocks = -(-n_assign // bm) + ne
    blk_start = jnp.arange(n_blocks, dtype=jnp.int32) * bm
    block_e = jnp.minimum(jnp.sum((pends[None, :] <= blk_start[:, None]).astype(jnp.int32), axis=1), ne - 1)
    n_used = (pends[ne - 1:] // bm).astype(jnp.int32)
    blk_off = blk_start - pstarts[block_e]
    n_valid = jnp.clip(counts[block_e] - blk_off, 0, bm).astype(jnp.int32)
    off = blk_off[:, None] + jnp.arange(bm, dtype=jnp.int32)[None, :]
    src = jnp.clip(starts[block_e][:, None] + off, 0, n_assign - 1)
    row_slot = jnp.where(off < counts[block_e][:, None], order[src.reshape(-1)].reshape(n_blocks, bm), 0)
    y = _expert_mlp(block_e, n_valid, n_used, row_slot.reshape(-1), h, w_gate, w_up, w_down, layer)
    return y, w


def _rope_tables(pos, rope):
    inv = 1.0 / (ROPE_THETA ** (jnp.arange(0, rope, 2, dtype=F32) / rope))
    ang = pos.astype(F32)[:, None] * inv[None, :]
    cos, sin = jnp.cos(ang), jnp.sin(ang)
    pad = jnp.zeros((pos.shape[0], LANES - rope), F32)
    return jnp.concatenate([cos, cos, pad], axis=1), jnp.concatenate([-sin, sin, pad], axis=1)


def kernel(x_prompt, x_sample, c_prompt, c_sample, state_ssm_re, state_ssm_im, cache_ckv, cache_krope, w_ada, b_ada, ln_g, ln_b, ssm_a_re, ssm_a_im, ssm_log_dt, ssm_b_re, ssm_b_im, ssm_c_re, ssm_c_im, ssm_d, ssm_w_glu, mla_w_in, mla_q_norm, mla_kv_norm, mla_w_uq, mla_w_uk, mla_w_uv, mla_w_o, w_router, router_bias, moe_w_gate, moe_w_up, moe_w_down):
    depth, d = w_ada.shape[0], w_ada.shape[1]
    alpha = (2 * depth) ** 0.25
    nh, nope, vd = mla_w_uk.shape[2], mla_w_uk.shape[3], mla_w_uv.shape[3]
    ql, kvl = mla_q_norm.shape[1], mla_kv_norm.shape[1]
    rope = mla_w_uq.shape[3] - nope
    assert nope == LANES and vd == LANES and rope <= LANES
    scale = (nope + rope) ** -0.5 * math.log2(math.e)
    n_groups, n_state = ssm_a_re.shape[1], ssm_a_re.shape[2]
    past = cache_ckv.shape[2]

    xs = [x_prompt, x_sample]
    bsz = [x.shape[0] for x in xs]
    seq = [x.shape[1] for x in xs]
    row0 = [0, bsz[0] * seq[0]]
    n_tokens = row0[1] + bsz[1] * seq[1]
    pos = [jnp.arange(seq[0]), past + jnp.arange(seq[1])]
    tabs = [_rope_tables(p, rope) for p in pos]

    mod = _ada_mod(jnp.concatenate([c_prompt, c_sample], axis=0), w_ada, b_ada)

    def mod_rows(i, grp, piece):
        b0 = 0 if grp == 0 else bsz[0]
        return mod[i, b0:b0 + bsz[grp], piece * d:(piece + 1) * d].reshape(bsz[grp], 1, d)

    assert depth % 2 == 0, "layers alternate S5 / MLA mixers, starting with S5"
    w_router_b = w_router.astype(BF16)
    half = rope // 2
    ssm_out = [[[], []], [[], []]]
    ckv_out, kr_out = [[], []], [[], []]

    h_in = [None, None]
    for i in range(depth):
        j = i // 2
        x1s, tokens = [], jnp.zeros((n_tokens, d), F32)
        if i % 2 == 0:
            bd, cd, ab = _s5_tables(ssm_a_re[j], ssm_a_im[j], ssm_log_dt[j], ssm_b_re[j], ssm_b_im[j],
                                    ssm_c_re[j], ssm_c_im[j])
            w_glu = ssm_w_glu[j].astype(BF16)
        else:
            w_in = mla_w_in[j]
            r0 = ql + kvl
            zc = jnp.zeros((d, LANES - rope), F32)
            w_in_ext = jnp.concatenate([w_in, zc, w_in[:, r0 + half:], w_in[:, r0:r0 + half], zc], axis=1).astype(BF16)
            wq = mla_w_uq[j]
            zq = jnp.zeros((ql, nh, LANES - rope), F32)
            wq_rope = jnp.concatenate([wq[..., nope:], zq], axis=-1)
            wq_rot = jnp.concatenate([wq[..., nope + half:], wq[..., nope:nope + half], zq], axis=-1)
            wq = jnp.concatenate([wq[..., :nope].reshape(ql, -1), wq_rope.reshape(ql, -1),
                                  wq_rot.reshape(ql, -1)], axis=1).astype(BF16)
            wkv = jnp.concatenate([mla_w_uk[j].reshape(kvl, -1), mla_w_uv[j].reshape(kvl, -1)],
                                  axis=1).astype(BF16)
            w_ukt = mla_w_uk[j].transpose(1, 2, 0).astype(BF16)
            w_uvh = mla_w_uv[j].transpose(1, 0, 2).astype(BF16)
            w_o = mla_w_o[j].astype(BF16)
        for grp in range(2):
            x = xs[grp]
            b, l = bsz[grp], seq[grp]
            sh1, sc1, g1, sh2, sc2 = (mod_rows(i, grp, p) for p in range(5))
            lng, lnb = ln_g[i, 0].reshape(1, d), ln_b[i, 0].reshape(1, d)
            if i % 2 == 0:
                nbg = b // SUBLANES
                if grp == 0:
                    h0r = jnp.zeros((nbg, SUBLANES, n_groups * n_state), F32)
                    h0i = h0r
                else:
                    h0r = state_ssm_re[j].reshape(nbg, SUBLANES, n_groups * n_state)
                    h0i = state_ssm_im[j].reshape(nbg, SUBLANES, n_groups * n_state)
                z, hr, hi = _s5_scan(x, sc1, sh1, bd, cd, ab, ssm_d[j].reshape(1, d), h0r, h0i)
                ssm_out[grp][0].append(hr.reshape(b, n_groups, n_state))
                ssm_out[grp][1].append(hi.reshape(b, n_groups, n_state))

                def a_spec(tm):
                    return pl.BlockSpec((None, tm, d), lambda bi, t: (bi, t, 0))

                x1, tokens = _mix_ln(z, a_spec, w_glu, x, g1, lng, lnb, sc2, sh2, tokens, row0[grp],
                                     glu=True, alpha=alpha)
            else:
                cos_t, sin_t = tabs[grp]
                cq, ckv, kr, krp = _mla_in(h_in[grp], w_in_ext, mla_q_norm[j].reshape(1, ql),
                                           mla_kv_norm[j].reshape(1, kvl), cos_t, sin_t, ql=ql, kvl=kvl, rope=rope)
                q = _mla_q(cq, wq, cos_t, sin_t, nh=nh, scale=scale)
                if grp == 0:
                    kn, v = _mla_kv(ckv, wkv, nh=nh)
                    o = _prompt_attn(q, kn, krp, v)
                else:
                    q_lat = _absorb_q(q, w_ukt, nope=nope)
                    o_lat = _cached_attn(q_lat, q, cache_ckv[j], cache_krope[j], ckv, kr, nope=nope, rope=rope)
                    o = _absorb_v(o_lat, w_uvh)
                ckv_out[grp].append(ckv)
                kr_out[grp].append(kr)

                def a_spec(tm):
                    return pl.BlockSpec((None, tm, nh * vd), lambda bi, t: (bi, t, 0))

                x1, tokens = _mix_ln(o, a_spec, w_o, x, g1, lng, lnb, sc2, sh2, tokens, row0[grp],
                                     glu=False, alpha=alpha)
            x1s.append(x1)
        y_all, wt_all = _moe(tokens, w_router_b, router_bias, moe_w_gate, moe_w_up, moe_w_down, i)
        for grp in range(2):
            g2 = mod_rows(i, grp, 5)
            next_mod = (mod_rows(i + 1, grp, 1), mod_rows(i + 1, grp, 0)) if i % 2 == 0 else None
            xs[grp], h_in[grp] = _ffn_ln(y_all, wt_all, row0[grp], x1s[grp], g2, ln_g[i, 1].reshape(1, d),
                                         ln_b[i, 1].reshape(1, d), next_mod, alpha=alpha)

    return (xs[0], xs[1],
            jnp.stack(ssm_out[0][0]), jnp.stack(ssm_out[0][1]), jnp.stack(ckv_out[0]), jnp.stack(kr_out[0]),
            jnp.stack(ssm_out[1][0]), jnp.stack(ssm_out[1][1]), jnp.stack(ckv_out[1]), jnp.stack(kr_out[1]))
```

```python
import functools
import math

import jax
import jax.numpy as jnp
from jax import lax
from jax.experimental import pallas as pl
from jax.experimental.pallas import tpu as pltpu

F32 = jnp.float32
BF16 = jnp.bfloat16

CHUNK = 64
SSM_GROUP_WIDTH = 16
N_EXPERT_GROUPS = 8
TOP_K = 2
ROPE_THETA = 10000.0
LN_EPS = 1e-5
RMS_EPS = 1e-6
NEG_INF = -1e30

V7X_VMEM_BYTES = 64 * 1024 * 1024
SUBLANES = 8
LANES = 128
MXU_DIM = 256

VMEM_LIMIT = V7X_VMEM_BYTES * 7 // 8
S5_CHANNELS = MXU_DIM
S5_TIME = 256
ROW_TILE = 256
ATTN_BLOCK = 512
ATTN_HEADS = 4
MOE_ROWS = 256
CACHE_KEYS = 1024


def _cparams(sem):
    return pltpu.CompilerParams(dimension_semantics=sem, vmem_limit_bytes=VMEM_LIMIT)


def _bdot(a, b):
    return jnp.dot(a, b, preferred_element_type=F32)


def _ada_kernel(c_ref, w_ref, b_ref, o_ref):
    c = c_ref[...]
    cs = (c * jax.nn.sigmoid(c)).astype(BF16)
    o_ref[...] = _bdot(cs, w_ref[...].astype(BF16)) + b_ref[...]


def _ada_mod(c_all, w_ada, b_ada):
    depth, d, n = w_ada.shape
    bc = c_all.shape[0]
    tn = 1024
    return pl.pallas_call(
        _ada_kernel,
        grid=(depth, n // tn),
        in_specs=[pl.BlockSpec((bc, d), lambda i, j: (0, 0)),
                  pl.BlockSpec((None, d, tn), lambda i, j: (i, 0, j)),
                  pl.BlockSpec((None, 1, tn), lambda i, j: (i, 0, j))],
        out_specs=pl.BlockSpec((None, bc, tn), lambda i, j: (i, 0, j)),
        out_shape=jax.ShapeDtypeStruct((depth, bc, n), F32),
        compiler_params=_cparams(("parallel", "parallel")),
        name="ada_mod",
    )(c_all, w_ada, b_ada.reshape(depth, 1, n))


def _s5_kernel(x_ref, sc_ref, sh_ref, bd_ref, cd_ref, ab_ref, d_ref, h0r_ref, h0i_ref,
               z_ref, hr_ref, hi_ref, u_ref, xs_ref, hs_ref, h_ref, *, tt, sb):
    t = pl.program_id(2)

    @pl.when(t == 0)
    def _():
        h_ref[0] = h0r_ref[...]
        h_ref[1] = h0i_ref[...]

    lane_tiles = range(u_ref.shape[0])
    for b in range(SUBLANES):
        ub = x_ref[b] * (1.0 + sc_ref[b]) + sh_ref[b]
        for c in lane_tiles:
            u_ref[c, pl.ds(b, tt, stride=SUBLANES), :] = ub[:, c * LANES:(c + 1) * LANES]
    u = jnp.concatenate([u_ref[c] for c in lane_tiles], axis=1)
    xs_ref[...] = _bdot(u.astype(BF16), bd_ref[...])
    a_re = jnp.broadcast_to(ab_ref[0:1, :], (SUBLANES, sb))
    a_im = jnp.broadcast_to(ab_ref[1:2, :], (SUBLANES, sb))

    pair = 2 * SUBLANES

    def step(i, carry):
        h_re, h_im = carry
        r0 = pl.multiple_of(i * pair, pair)
        res, ims = [], []
        for s in range(2):
            x_re = xs_ref[pl.ds(r0 + s * SUBLANES, SUBLANES), 0:sb]
            x_im = xs_ref[pl.ds(r0 + s * SUBLANES, SUBLANES), sb:2 * sb]
            h_re, h_im = a_re * h_re - a_im * h_im + x_re, a_re * h_im + a_im * h_re + x_im
            res.append(h_re)
            ims.append(h_im)
        hs_ref[pl.ds(r0, pair), 0:sb] = jnp.concatenate(res, axis=0).astype(BF16)
        hs_ref[pl.ds(r0, pair), sb:2 * sb] = jnp.concatenate(ims, axis=0).astype(BF16)
        return h_re, h_im

    h_re, h_im = lax.fori_loop(0, tt // 2, step, (h_ref[0], h_ref[1]))
    h_ref[0] = h_re
    h_ref[1] = h_im
    y = _bdot(hs_ref[...], cd_ref[...]) + d_ref[...] * u
    zf = jax.nn.gelu(y)
    for c in lane_tiles:
        u_ref[c] = zf[:, c * LANES:(c + 1) * LANES]
    for b in range(SUBLANES):
        for c in lane_tiles:
            z_ref[b, :, c * LANES:(c + 1) * LANES] = (
                u_ref[c, pl.ds(b, tt, stride=SUBLANES), :].astype(z_ref.dtype))

    @pl.when(t == pl.num_programs(2) - 1)
    def _():
        hr_ref[...] = h_re
        hi_ref[...] = h_im


def _s5_scan(x, sc, sh, bd, cd, ab, d_skip, h0_re, h0_im):
    b, l, d = x.shape
    nbg = b // SUBLANES
    nj, cb, sb2 = bd.shape
    sb = sb2 // 2
    tt = min(l, S5_TIME)
    kern = functools.partial(_s5_kernel, tt=tt, sb=sb)
    tile = pl.BlockSpec((SUBLANES, tt, cb), lambda g, j, t: (g, t, j))
    row = pl.BlockSpec((SUBLANES, 1, cb), lambda g, j, t: (g, 0, j))
    state_spec = pl.BlockSpec((None, SUBLANES, sb), lambda g, j, t: (g, 0, j))
    state_shape = jax.ShapeDtypeStruct((nbg, SUBLANES, nj * sb), F32)
    return pl.pallas_call(
        kern,
        grid=(nbg, nj, l // tt),
        in_specs=[tile, row, row,
                  pl.BlockSpec((None, cb, sb2), lambda g, j, t: (j, 0, 0)),
                  pl.BlockSpec((None, sb2, cb), lambda g, j, t: (j, 0, 0)),
                  pl.BlockSpec((None, 2, sb), lambda g, j, t: (j, 0, 0)),
                  pl.BlockSpec((1, cb), lambda g, j, t: (0, j)),
                  state_spec, state_spec],
        out_specs=[tile, state_spec, state_spec],
        out_shape=[jax.ShapeDtypeStruct((b, l, d), BF16), state_shape, state_shape],
        scratch_shapes=[pltpu.VMEM((cb // LANES, tt * SUBLANES, LANES), F32), pltpu.VMEM((tt * SUBLANES, sb2), F32),
                        pltpu.VMEM((tt * SUBLANES, sb2), BF16), pltpu.VMEM((2, SUBLANES, sb), F32)],
        compiler_params=_cparams(("parallel", "parallel", "arbitrary")),
        name="s5_scan",
    )(x, sc, sh, bd, cd, ab, d_skip, h0_re, h0_im)


def _s5_tables(a_re, a_im, log_dt, b_re, b_im, c_re, c_im):
    g, p = a_re.shape
    w = b_re.shape[-1]
    gl = S5_CHANNELS // w
    nj = g // gl
    dt = jnp.exp(log_dt)[:, None]
    mag = jnp.exp(a_re * dt)
    abar_re, abar_im = mag * jnp.cos(a_im * dt), mag * jnp.sin(a_im * dt)
    den = a_re * a_re + a_im * a_im
    n_re, n_im = abar_re - 1.0, abar_im
    f_re = (n_re * a_re + n_im * a_im) / den
    f_im = (n_im * a_re - n_re * a_im) / den
    bb_re = f_re[..., None] * b_re - f_im[..., None] * b_im
    bb_im = f_re[..., None] * b_im + f_im[..., None] * b_re
    eye = jnp.eye(gl, dtype=F32)
    bb = jnp.stack([bb_re, bb_im]).reshape(2, nj, gl, p, w)
    bd = jnp.einsum('sjgpw,gh->jgwshp', bb, eye).reshape(nj, gl * w, 2 * gl * p)
    cc = jnp.stack([c_re, -c_im]).reshape(2, nj, gl, w, p)
    cd = jnp.einsum('sjgwp,gh->jsgphw', cc, eye).reshape(nj, 2 * gl * p, gl * w)
    ab = jnp.stack([abar_re.reshape(nj, gl * p), abar_im.reshape(nj, gl * p)], axis=1)
    return bd.astype(BF16), cd.astype(BF16), ab


def _layer_norm(y, g, b):
    mu = jnp.mean(y, axis=-1, keepdims=True)
    yc = y - mu
    var = jnp.mean(yc * yc, axis=-1, keepdims=True)
    return yc * lax.rsqrt(var + LN_EPS) * g + b


def _mix_ln_kernel(a_ref, w_ref, x_ref, gate_ref, lng_ref, lnb_ref, sc_ref, sh_ref, tokens_hbm,
                   x1_ref, h_ref, *, glu, alpha):
    del tokens_hbm
    acc = _bdot(a_ref[...], w_ref[...])
    if glu:
        d = acc.shape[-1] // 2
        m = acc[:, :d] * jax.nn.sigmoid(acc[:, d:])
    else:
        m = acc
    x1 = _layer_norm(alpha * x_ref[...] + gate_ref[...] * m, lng_ref[...], lnb_ref[...])
    x1_ref[...] = x1
    h_ref[...] = x1 * (1.0 + sc_ref[...]) + sh_ref[...]


def _mix_ln(a, a_spec, w, x, gate, ln_g, ln_b, sc, sh, tokens, row0, *, glu, alpha):
    b, l, d = x.shape
    tm = min(l, ROW_TILE)
    nt = l // tm
    blk0 = row0 // tm
    k, n = w.shape
    row = pl.BlockSpec((None, 1, d), lambda i, t: (i, 0, 0))
    vec = pl.BlockSpec((1, d), lambda i, t: (0, 0))
    tile = pl.BlockSpec((None, tm, d), lambda i, t: (i, t, 0))
    return pl.pallas_call(
        functools.partial(_mix_ln_kernel, glu=glu, alpha=alpha),
        grid=(b, nt),
        in_specs=[a_spec(tm), pl.BlockSpec((k, n), lambda i, t: (0, 0), pipeline_mode=pl.Buffered(1)),
                  tile, row, vec, vec, row, row, pl.BlockSpec(memory_space=pl.ANY)],
        out_specs=[tile, pl.BlockSpec((tm, d), lambda i, t: (blk0 + i * nt + t, 0))],
        out_shape=[jax.ShapeDtypeStruct((b, l, d), F32), jax.ShapeDtypeStruct(tokens.shape, F32)],
        input_output_aliases={8: 1},
        compiler_params=_cparams(("parallel", "parallel")),
        name="mix_ln_glu" if glu else "mix_ln_proj",
    )(a, w, x, gate, ln_g, ln_b, sc, sh, tokens)


def _ffn_ln_kernel(*refs, alpha):
    y_refs, (wt_ref, x_ref, gate_ref, lng_ref, lnb_ref), rest = refs[:TOP_K], refs[TOP_K:TOP_K + 5], refs[TOP_K + 5:]
    f = wt_ref[:, 0:1] * y_refs[0][...]
    for k in range(1, TOP_K):
        f = f + wt_ref[:, k:k + 1] * y_refs[k][...]
    x2 = _layer_norm(alpha * x_ref[...] + gate_ref[...] * f, lng_ref[...], lnb_ref[...])
    if len(rest) == 1:
        rest[0][...] = x2
    else:
        sc_ref, sh_ref, x2_ref, h_ref = rest
        x2_ref[...] = x2
        h_ref[...] = (x2 * (1.0 + sc_ref[...]) + sh_ref[...]).astype(h_ref.dtype)


def _ffn_ln(y_all, wt_all, row0, x, gate, ln_g, ln_b, next_mod, *, alpha):
    b, l, d = x.shape
    tm = min(l, ROW_TILE)
    nt = l // tm
    blk0 = row0 // tm
    kblk = wt_all.shape[0] // tm
    row = pl.BlockSpec((None, 1, d), lambda i, t: (i, 0, 0))
    vec = pl.BlockSpec((1, d), lambda i, t: (0, 0))
    tile = pl.BlockSpec((None, tm, d), lambda i, t: (i, t, 0))
    in_specs = [pl.BlockSpec((tm, d), functools.partial(lambda k, i, t: (k * kblk + blk0 + i * nt + t, 0), k))
                for k in range(TOP_K)]
    in_specs += [pl.BlockSpec((tm, TOP_K), lambda i, t: (blk0 + i * nt + t, 0)), tile, row, vec, vec]
    x2_shape = jax.ShapeDtypeStruct((b, l, d), F32)
    if next_mod is None:
        args, out_specs, out_shape = (), tile, x2_shape
    else:
        args, in_specs = tuple(next_mod), in_specs + [row, row]
        out_specs, out_shape = [tile, tile], [x2_shape, jax.ShapeDtypeStruct((b, l, d), BF16)]
    out = pl.pallas_call(
        functools.partial(_ffn_ln_kernel, alpha=alpha),
        grid=(b, nt),
        in_specs=in_specs,
        out_specs=out_specs,
        out_shape=out_shape,
        compiler_params=_cparams(("parallel", "parallel")),
        name="ffn_ln",
    )(*([y_all] * TOP_K), wt_all, x, gate, ln_g, ln_b, *args)
    return (out, None) if next_mod is None else out


def _rms_norm(x, g):
    ms = jnp.mean(x * x, axis=-1, keepdims=True)
    return x * lax.rsqrt(ms + RMS_EPS) * g


def _mla_in_kernel(h_ref, w_ref, qn_ref, kvn_ref, cos_ref, sin_ref,
                   cq_ref, ckv_ref, kr_ref, krp_ref, *, ql, kvl, rope):
    acc = _bdot(h_ref[...], w_ref[...])
    cq_ref[...] = _rms_norm(acc[:, :ql], qn_ref[...]).astype(cq_ref.dtype)
    ckv_ref[...] = _rms_norm(acc[:, ql:ql + kvl], kvn_ref[...])
    o = ql + kvl
    kr = acc[:, o:o + LANES] * cos_ref[...] + acc[:, o + LANES:o + 2 * LANES] * sin_ref[...]
    kr_ref[...] = kr[:, :rope]
    krp_ref[...] = kr.astype(krp_ref.dtype)


def _mla_in(h, w_in_ext, q_norm, kv_norm, cos_t, sin_t, *, ql, kvl, rope):
    b, l, d = h.shape
    tm = min(l, ROW_TILE)
    n = w_in_ext.shape[1]
    kern = functools.partial(_mla_in_kernel, ql=ql, kvl=kvl, rope=rope)
    tab = pl.BlockSpec((tm, LANES), lambda i, t: (t, 0))

    def out(nf):
        return pl.BlockSpec((None, tm, nf), lambda i, t: (i, t, 0))

    return pl.pallas_call(
        kern,
        grid=(b, l // tm),
        in_specs=[pl.BlockSpec((None, tm, d), lambda i, t: (i, t, 0)),
                  pl.BlockSpec((d, n), lambda i, t: (0, 0)),
                  pl.BlockSpec((1, ql), lambda i, t: (0, 0)),
                  pl.BlockSpec((1, kvl), lambda i, t: (0, 0)), tab, tab],
        out_specs=[out(ql), out(kvl), out(rope), out(LANES)],
        out_shape=[jax.ShapeDtypeStruct((b, l, ql), BF16), jax.ShapeDtypeStruct((b, l, kvl), F32),
                   jax.ShapeDtypeStruct((b, l, rope), F32), jax.ShapeDtypeStruct((b, l, LANES), BF16)],
        compiler_params=_cparams(("parallel", "parallel")),
        name="mla_in",
    )(h, w_in_ext, q_norm, kv_norm, cos_t, sin_t)


def _mla_q_kernel(cq_ref, w_ref, cos_ref, sin_ref, q_ref, *, nh, scale):
    acc = _bdot(cq_ref[...], w_ref[...])
    cos = cos_ref[...] * scale
    sin = sin_ref[...] * scale
    for h in range(nh):
        q_ref[h, :, :LANES] = (acc[:, h * LANES:(h + 1) * LANES] * scale).astype(q_ref.dtype)
        r0, p0 = (nh + h) * LANES, (2 * nh + h) * LANES
        q_ref[h, :, LANES:] = (acc[:, r0:r0 + LANES] * cos + acc[:, p0:p0 + LANES] * sin).astype(q_ref.dtype)


def _mla_q(cq, wq, cos_t, sin_t, *, nh, scale):
    b, l, ql = cq.shape
    n = wq.shape[1]
    tm = min(l, ROW_TILE)
    tab = pl.BlockSpec((tm, LANES), lambda i, t: (t, 0))
    return pl.pallas_call(
        functools.partial(_mla_q_kernel, nh=nh, scale=scale),
        grid=(b, l // tm),
        in_specs=[pl.BlockSpec((None, tm, ql), lambda i, t: (i, t, 0)),
                  pl.BlockSpec((ql, n), lambda i, t: (0, 0)), tab, tab],
        out_specs=pl.BlockSpec((None, nh, tm, 2 * LANES), lambda i, t: (i, 0, t, 0)),
        out_shape=jax.ShapeDtypeStruct((b, nh, l, 2 * LANES), BF16),
        compiler_params=_cparams(("parallel", "parallel")),
        name="mla_q",
    )(cq, wq, cos_t, sin_t)


def _mla_kv_kernel(ckv_ref, w_ref, k_ref, v_ref, *, nh):
    acc = _bdot(ckv_ref[...].astype(BF16), w_ref[...])
    for h in range(nh):
        k_ref[h] = acc[:, h * LANES:(h + 1) * LANES].astype(k_ref.dtype)
        v_ref[h] = acc[:, (nh + h) * LANES:(nh + h + 1) * LANES].astype(v_ref.dtype)


def _mla_kv(ckv, wkv, *, nh):
    b, l, kvl = ckv.shape
    n = wkv.shape[1]
    tm = min(l, ROW_TILE)
    head_tile = pl.BlockSpec((None, nh, tm, LANES), lambda i, t: (i, 0, t, 0))
    head_shape = jax.ShapeDtypeStruct((b, nh, l, LANES), BF16)
    return pl.pallas_call(
        functools.partial(_mla_kv_kernel, nh=nh),
        grid=(b, l // tm),
        in_specs=[pl.BlockSpec((None, tm, kvl), lambda i, t: (i, t, 0)),
                  pl.BlockSpec((kvl, n), lambda i, t: (0, 0))],
        out_specs=[head_tile, head_tile],
        out_shape=[head_shape, head_shape],
        compiler_params=_cparams(("parallel", "parallel")),
        name="mla_kv",
    )(ckv, wkv)


def _qk(q, k):
    return lax.dot_general(q, k, (((1,), (1,)), ((), ())), preferred_element_type=F32)


def _chunk_of(pos):
    if CHUNK & (CHUNK - 1) == 0:
        return lax.shift_right_logical(pos, CHUNK.bit_length() - 1)
    return pos // CHUNK


def _softmax_step(carry, s, v):
    m, l, acc = carry
    m_new = jnp.maximum(m, jnp.max(s, axis=-1, keepdims=True))
    p = jnp.exp2(s - m_new)
    alpha = jnp.exp2(m - m_new)
    l = alpha * l + jnp.sum(p, axis=-1, keepdims=True)
    acc = alpha * acc + _bdot(p.astype(BF16), v)
    return m_new, l, acc


def _softmax_step_ones(carry, s, v):
    m, acc = carry
    m_new = jnp.maximum(m, jnp.max(s, axis=-1, keepdims=True))
    p = jnp.exp2(s - m_new)
    acc = jnp.exp2(m - m_new) * acc + _bdot(p.astype(BF16), v)
    return m_new, acc


def _prompt_attn_kernel(q_ref, kn_ref, kr_ref, v_ref, o_ref, *, blk, vd):
    i = pl.program_id(2)
    heads = range(q_ref.shape[0])
    qs = [q_ref[h] for h in heads]
    lane = lax.broadcasted_iota(jnp.int32, (blk, LANES), 1)
    ones_col = jnp.where(lane == 0, 1.0, 0.0).astype(v_ref.dtype)

    def keys(h, r):
        return jnp.concatenate([kn_ref[h, r, :], kr_ref[r, :]], axis=1)

    def values(h, r):
        return jnp.concatenate([v_ref[h, r, :], ones_col], axis=1)

    def body(j, carry):
        r = pl.ds(pl.multiple_of(j * blk, blk), blk)
        return tuple(_softmax_step_ones(carry[h], _qk(qs[h], keys(h, r)), values(h, r)) for h in heads)

    init = (jnp.full((blk, 1), NEG_INF, F32), jnp.zeros((blk, 2 * LANES), F32))
    carry = lax.fori_loop(0, i, body, tuple(init for _ in heads))
    r = pl.ds(pl.multiple_of(i * blk, blk), blk)
    q_chunk = _chunk_of(lax.broadcasted_iota(jnp.int32, (blk, blk), 0))
    k_chunk = _chunk_of(lax.broadcasted_iota(jnp.int32, (blk, blk), 1))
    visible = k_chunk <= q_chunk
    for h in heads:
        s = jnp.where(visible, _qk(qs[h], keys(h, r)), NEG_INF)
        _, acc = _softmax_step_ones(carry[h], s, values(h, r))
        o_ref[:, h * vd:(h + 1) * vd] = (acc[:, :vd] / acc[:, vd:vd + 1]).astype(o_ref.dtype)


def _prompt_attn(q, kn, kr, v):
    b, nh, l, dk = q.shape
    vd = v.shape[-1]
    blk = min(l, ATTN_BLOCK)
    hp = ATTN_HEADS
    per_head = pl.BlockSpec((None, hp, l, LANES), lambda i, h, t: (i, h, 0, 0))
    return pl.pallas_call(
        functools.partial(_prompt_attn_kernel, blk=blk, vd=vd),
        grid=(b, nh // hp, l // blk),
        in_specs=[pl.BlockSpec((None, hp, blk, dk), lambda i, h, t: (i, h, t, 0)), per_head,
                  pl.BlockSpec((None, l, LANES), lambda i, h, t: (i, 0, 0)), per_head],
        out_specs=pl.BlockSpec((None, blk, hp * vd), lambda i, h, t: (i, t, h)),
        out_shape=jax.ShapeDtypeStruct((b, l, nh * vd), BF16),
        compiler_params=_cparams(("parallel", "parallel", "arbitrary")),
        name="prompt_attn",
    )(q, kn, kr, v)


def _absorb_q_kernel(q_ref, w_ref, o_ref, *, nope):
    b, lq, dk = q_ref.shape
    q = q_ref[...].reshape(b * lq, dk)[:, :nope]
    o_ref[...] = _bdot(q, w_ref[...]).reshape(b, lq, -1).astype(o_ref.dtype)


def _absorb_q(q, w_ukt, *, nope):
    b, nh, lq, dk = q.shape
    kvl = w_ukt.shape[-1]
    return pl.pallas_call(
        functools.partial(_absorb_q_kernel, nope=nope),
        grid=(nh,),
        in_specs=[pl.BlockSpec((b, None, lq, dk), lambda h: (0, h, 0, 0)),
                  pl.BlockSpec((None, nope, kvl), lambda h: (h, 0, 0))],
        out_specs=pl.BlockSpec((b, None, lq, kvl), lambda h: (0, h, 0, 0)),
        out_shape=jax.ShapeDtypeStruct((b, nh, lq, kvl), BF16),
        compiler_params=_cparams(("parallel",)),
        name="absorb_q",
    )(q, w_ukt)


def _cached_attn_kernel(ql_ref, q_ref, cc_ref, ck_ref, nc_ref, nk_ref, o_ref, *, nope, rope, past, kb):
    nh, lq, kvl = ql_ref.shape
    rows = nh * lq
    ql = ql_ref[...].reshape(rows, kvl)
    qr = q_ref[...].reshape(rows, -1)[:, nope:nope + rope]
    q_pos = past + lax.broadcasted_iota(jnp.int32, (rows, 1), 0) % lq
    carry = (jnp.full((rows, 1), NEG_INF, F32), jnp.zeros((rows, 1), F32), jnp.zeros((rows, kvl), F32))

    def scores(c, kr, k0):
        s = _qk(ql, c) + _qk(qr, kr)
        k_pos = k0 + lax.broadcasted_iota(jnp.int32, s.shape, 1)
        return jnp.where(_chunk_of(k_pos) <= _chunk_of(q_pos), s, NEG_INF)

    for j in range(past // kb):
        c = cc_ref[j * kb:(j + 1) * kb, :].astype(BF16)
        kr = ck_ref[j * kb:(j + 1) * kb, :].astype(BF16)
        carry = _softmax_step(carry, scores(c, kr, j * kb), c)
    c = nc_ref[...].astype(BF16)
    kr = nk_ref[...].astype(BF16)
    _, l, acc = _softmax_step(carry, scores(c, kr, past), c)
    o_ref[...] = (acc / l).reshape(nh, lq, kvl).astype(o_ref.dtype)


def _cached_attn(q_lat, q, cache_ckv, cache_kr, new_ckv, new_kr, *, nope, rope):
    b, nh, lq, kvl = q_lat.shape
    past = cache_ckv.shape[1]
    kb = min(past, CACHE_KEYS)
    dk = q.shape[-1]
    kern = functools.partial(_cached_attn_kernel, nope=nope, rope=rope, past=past, kb=kb)
    return pl.pallas_call(
        kern,
        grid=(b,),
        in_specs=[pl.BlockSpec((None, nh, lq, kvl), lambda i: (i, 0, 0, 0)),
                  pl.BlockSpec((None, nh, lq, dk), lambda i: (i, 0, 0, 0)),
                  pl.BlockSpec((None, past, kvl), lambda i: (i, 0, 0)),
                  pl.BlockSpec((None, past, rope), lambda i: (i, 0, 0)),
                  pl.BlockSpec((None, lq, kvl), lambda i: (i, 0, 0)),
                  pl.BlockSpec((None, lq, rope), lambda i: (i, 0, 0))],
        out_specs=pl.BlockSpec((None, nh, lq, kvl), lambda i: (i, 0, 0, 0)),
        out_shape=jax.ShapeDtypeStruct((b, nh, lq, kvl), BF16),
        compiler_params=_cparams(("parallel",)),
        name="cached_attn",
    )(q_lat, q, cache_ckv, cache_kr, new_ckv, new_kr)


def _absorb_v_kernel(o_ref, w_ref, out_ref):
    b, lq, kvl = o_ref.shape
    out_ref[...] = _bdot(o_ref[...].reshape(b * lq, kvl), w_ref[...]).reshape(b, lq, -1).astype(out_ref.dtype)


def _absorb_v(o_lat, w_uv):
    b, nh, lq, kvl = o_lat.shape
    vd = w_uv.shape[-1]
    return pl.pallas_call(
        _absorb_v_kernel,
        grid=(nh,),
        in_specs=[pl.BlockSpec((b, None, lq, kvl), lambda h: (0, h, 0, 0)),
                  pl.BlockSpec((None, kvl, vd), lambda h: (h, 0, 0))],
        out_specs=pl.BlockSpec((b, lq, vd), lambda h: (0, 0, h)),
        out_shape=jax.ShapeDtypeStruct((b, lq, nh * vd), BF16),
        compiler_params=_cparams(("parallel",)),
        name="absorb_v",
    )(o_lat, w_uv)


def _router_kernel(x_ref, w_ref, o_ref):
    o_ref[...] = _bdot(x_ref[...].astype(BF16), w_ref[...])


def _router_logits(x, w):
    t, d = x.shape
    e = w.shape[1]
    tm = math.gcd(t, 1024)
    return pl.pallas_call(
        _router_kernel,
        grid=(t // tm,),
        in_specs=[pl.BlockSpec((tm, d), lambda i: (i, 0)), pl.BlockSpec((d, e), lambda i: (0, 0))],
        out_specs=pl.BlockSpec((tm, e), lambda i: (i, 0)),
        out_shape=jax.ShapeDtypeStruct((t, e), F32),
        compiler_params=_cparams(("parallel",)),
        name="router_logits",
    )(x, w)


def _expert_kernel(be_ref, nv_ref, nb_ref, slot_ref, x_hbm, wg_ref, wu_ref, wd_ref, y_hbm,
                   xbuf, obuf, wgb_ref, wub_ref, wdb_ref, gsem, ssem, *, bm):
    i = pl.program_id(0)
    n_used = nb_ref[0]
    cur = lax.rem(i, 2)
    n_tok = x_hbm.shape[0]

    def token_of(slot):
        if TOP_K & (TOP_K - 1) == 0:
            return lax.shift_right_logical(slot, TOP_K.bit_length() - 1)
        return lax.div(slot, TOP_K)

    def start_gather(blk, buf):
        base = blk * bm
        for r in range(bm):
            tok = token_of(slot_ref[base + r])
            pltpu.make_async_copy(x_hbm.at[pl.ds(tok, 1), :], xbuf.at[buf, pl.ds(r, 1), :], gsem.at[buf]).start()

    def scatter_row(base, buf, r):
        slot = slot_ref[base + r]
        tok = token_of(slot)
        dst = (slot - tok * TOP_K) * n_tok + tok
        pltpu.make_async_copy(obuf.at[buf, pl.ds(r, 1), :], y_hbm.at[pl.ds(dst, 1), :], ssem.at[buf]).start()

    def start_scatter(blk, buf):
        base = blk * bm
        nv = nv_ref[blk]

        @pl.when(nv == bm)
        def _():
            for r in range(bm):
                scatter_row(base, buf, r)

        @pl.when(nv < bm)
        def _():
            def group(g, carry):
                r0 = pl.multiple_of(g * SUBLANES, SUBLANES)
                for j in range(SUBLANES):
                    scatter_row(base, buf, r0 + j)
                return carry
            full = lax.shift_right_logical(nv, SUBLANES.bit_length() - 1)
            lax.fori_loop(0, full, group, 0)

            def single(r, carry):
                scatter_row(base, buf, r)
                return carry
            lax.fori_loop(full * SUBLANES, nv, single, 0)

    def wait_scatter(blk, buf):
        n = nv_ref[blk]
        piece = bm
        while piece >= 1:
            @pl.when((n & piece) != 0)
            def _(piece=piece):
                rows, times = (piece, 1) if piece >= SUBLANES else (1, piece)
                for _ in range(times):
                    pltpu.make_async_copy(obuf.at[buf, pl.ds(0, rows), :], y_hbm.at[pl.ds(0, rows), :],
                                          ssem.at[buf]).wait()
            piece //= 2

    @pl.when(i == 0)
    def _():
        start_gather(0, 0)

    def wait_gather(buf):
        pltpu.make_async_copy(x_hbm.at[pl.ds(0, bm), :], xbuf.at[buf], gsem.at[buf]).wait()

    @pl.when(i < n_used)
    def _():
        wait_gather(cur)

        @pl.when(jnp.logical_or(i == 0, be_ref[i] != be_ref[jnp.maximum(i - 1, 0)]))
        def _():
            wgb_ref[...] = wg_ref[...].astype(BF16)
            wub_ref[...] = wu_ref[...].astype(BF16)
            wdb_ref[...] = wd_ref[...].astype(BF16)

        @pl.when(i >= 2)
        def _():
            wait_scatter(i - 2, cur)

        start_gather(jnp.minimum(i + 1, n_used - 1), 1 - cur)
        x = xbuf[cur].astype(BF16)
        g = _bdot(x, wgb_ref[...])
        h = (g * jax.nn.sigmoid(g)) * _bdot(x, wub_ref[...])
        obuf[cur] = _bdot(h.astype(BF16), wdb_ref[...])
        start_scatter(i, cur)

    @pl.when(i == pl.num_programs(0) - 1)
    def _():
        wait_gather(lax.rem(n_used, 2))

        @pl.when(n_used >= 2)
        def _():
            wait_scatter(n_used - 2, lax.rem(n_used, 2))
        wait_scatter(n_used - 1, lax.rem(n_used - 1, 2))


def _expert_mlp(block_e, n_valid, n_used, row_slot, x, w_gate, w_up, w_down, layer):
    t, d = x.shape
    ff = w_gate.shape[-1]
    bm = MOE_ROWS
    n_blocks = block_e.shape[0]
    grid_spec = pltpu.PrefetchScalarGridSpec(
        num_scalar_prefetch=4,
        grid=(n_blocks,),
        in_specs=[pl.BlockSpec(memory_space=pl.ANY),
                  pl.BlockSpec((None, None, d, ff), lambda i, be, nv, nb, sl: (layer, be[i], 0, 0)),
                  pl.BlockSpec((None, None, d, ff), lambda i, be, nv, nb, sl: (layer, be[i], 0, 0)),
                  pl.BlockSpec((None, None, ff, d), lambda i, be, nv, nb, sl: (layer, be[i], 0, 0))],
        out_specs=pl.BlockSpec(memory_space=pl.ANY),
        scratch_shapes=[pltpu.VMEM((2, bm, d), F32), pltpu.VMEM((2, bm, d), F32),
                        pltpu.VMEM((d, ff), BF16), pltpu.VMEM((d, ff), BF16), pltpu.VMEM((ff, d), BF16),
                        pltpu.SemaphoreType.DMA((2,)), pltpu.SemaphoreType.DMA((2,))],
    )
    return pl.pallas_call(
        functools.partial(_expert_kernel, bm=bm),
        grid_spec=grid_spec,
        out_shape=jax.ShapeDtypeStruct((TOP_K * t, d), F32),
        compiler_params=_cparams(("arbitrary",)),
        name="expert_mlp",
    )(block_e, n_valid, n_used, row_slot, x, w_gate, w_up, w_down)


def _first_argmax(x):
    return jnp.argmax(x, axis=-1).astype(jnp.int32)


def _moe(h, w_router, router_bias, w_gate, w_up, w_down, layer):
    t, d = h.shape
    ne = w_router.shape[1]
    ng = N_EXPERT_GROUPS
    eg = ne // ng
    scores = jax.nn.sigmoid(_router_logits(h, w_router))
    sel = scores + router_bias
    grp = sel.reshape(t, ng, eg)
    lane = jnp.arange(eg, dtype=jnp.int32)
    i1 = _first_argmax(grp)
    m1 = jnp.max(grp, axis=-1)
    m2 = jnp.max(jnp.where(lane == i1[..., None], -jnp.inf, grp), axis=-1)
    g_idx = _first_argmax(m1 + m2)
    sel_g = jnp.take_along_axis(grp, g_idx[:, None, None], axis=1)[:, 0]
    sc_g = jnp.take_along_axis(scores.reshape(t, ng, eg), g_idx[:, None, None], axis=1)[:, 0]
    e1 = _first_argmax(sel_g)
    e2 = _first_argmax(jnp.where(lane == e1[:, None], -jnp.inf, sel_g))
    w1 = jnp.take_along_axis(sc_g, e1[:, None], axis=1)[:, 0]
    w2 = jnp.take_along_axis(sc_g, e2[:, None], axis=1)[:, 0]
    wsum = w1 + w2
    e_idx = jnp.stack([g_idx * eg + e1, g_idx * eg + e2], axis=1)
    w = jnp.stack([w1 / wsum, w2 / wsum], axis=1)

    bm = MOE_ROWS
    n_assign = t * TOP_K
    flat_e = e_idx.reshape(-1)
    order = jnp.argsort(flat_e).astype(jnp.int32)
    counts = jnp.sum((flat_e[:, None] == jnp.arange(ne, dtype=jnp.int32)[None, :]).astype(jnp.int32), axis=0)
    starts = jnp.cumsum(counts) - counts
    pcounts = (counts + bm - 1) // bm * bm
    pends = jnp.cumsum(pcounts)
    pstarts = pends - pcounts
    n_blocks = -(-n_assign // bm) + ne
    blk_start = jnp.arange(n_blocks, dtype=jnp.int32) * bm
    block_e = jnp.minimum(jnp.sum((pends[None, :] <= blk_start[:, None]).astype(jnp.int32), axis=1), ne - 1)
    n_used = (pends[ne - 1:] // bm).astype(jnp.int32)
    blk_off = blk_start - pstarts[block_e]
    n_valid = jnp.clip(counts[block_e] - blk_off, 0, bm).astype(jnp.int32)
    off = blk_off[:, None] + jnp.arange(bm, dtype=jnp.int32)[None, :]
    src = jnp.clip(starts[block_e][:, None] + off, 0, n_assign - 1)
    row_slot = jnp.where(off < counts[block_e][:, None], order[src.reshape(-1)].reshape(n_blocks, bm), 0)
    y = _expert_mlp(block_e, n_valid, n_used, row_slot.reshape(-1), h, w_gate, w_up, w_down, layer)
    return y, w


def _rope_tables(pos, rope):
    inv = 1.0 / (ROPE_THETA ** (jnp.arange(0, rope, 2, dtype=F32) / rope))
    ang = pos.astype(F32)[:, None] * inv[None, :]
    cos, sin = jnp.cos(ang), jnp.sin(ang)
    pad = jnp.zeros((pos.shape[0], LANES - rope), F32)
    return jnp.concatenate([cos, cos, pad], axis=1), jnp.concatenate([-sin, sin, pad], axis=1)


def kernel(x_prompt, x_sample, c_prompt, c_sample, state_ssm_re, state_ssm_im, cache_ckv, cache_krope, w_ada, b_ada, ln_g, ln_b, ssm_a_re, ssm_a_im, ssm_log_dt, ssm_b_re, ssm_b_im, ssm_c_re, ssm_c_im, ssm_d, ssm_w_glu, mla_w_in, mla_q_norm, mla_kv_norm, mla_w_uq, mla_w_uk, mla_w_uv, mla_w_o, w_router, router_bias, moe_w_gate, moe_w_up, moe_w_down):
    depth, d = w_ada.shape[0], w_ada.shape[1]
    alpha = (2 * depth) ** 0.25
    nh, nope, vd = mla_w_uk.shape[2], mla_w_uk.shape[3], mla_w_uv.shape[3]
    ql, kvl = mla_q_norm.shape[1], mla_kv_norm.shape[1]
    rope = mla_w_uq.shape[3] - nope
    assert nope == LANES and vd == LANES and rope <= LANES
    scale = (nope + rope) ** -0.5 * math.log2(math.e)
    n_groups, n_state = ssm_a_re.shape[1], ssm_a_re.shape[2]
    past = cache_ckv.shape[2]

    xs = [x_prompt, x_sample]
    bsz = [x.shape[0] for x in xs]
    seq = [x.shape[1] for x in xs]
    row0 = [0, bsz[0] * seq[0]]
    n_tokens = row0[1] + bsz[1] * seq[1]
    pos = [jnp.arange(seq[0]), past + jnp.arange(seq[1])]
    tabs = [_rope_tables(p, rope) for p in pos]

    mod = _ada_mod(jnp.concatenate([c_prompt, c_sample], axis=0), w_ada, b_ada)

    def mod_rows(i, grp, piece):
        b0 = 0 if grp == 0 else bsz[0]
        return mod[i, b0:b0 + bsz[grp], piece * d:(piece + 1) * d].reshape(bsz[grp], 1, d)

    assert depth % 2 == 0, "layers alternate S5 / MLA mixers, starting with S5"
    w_router_b = w_router.astype(BF16)
    half = rope // 2
    ssm_out = [[[], []], [[], []]]
    ckv_out, kr_out = [[], []], [[], []]

    h_in = [None, None]
    for i in range(depth):
        j = i // 2
        x1s, tokens = [], jnp.zeros((n_tokens, d), F32)
        if i % 2 == 0:
            bd, cd, ab = _s5_tables(ssm_a_re[j], ssm_a_im[j], ssm_log_dt[j], ssm_b_re[j], ssm_b_im[j],
                                    ssm_c_re[j], ssm_c_im[j])
            w_glu = ssm_w_glu[j].astype(BF16)
        else:
            w_in = mla_w_in[j]
            r0 = ql + kvl
            zc = jnp.zeros((d, LANES - rope), F32)
            w_in_ext = jnp.concatenate([w_in, zc, w_in[:, r0 + half:], w_in[:, r0:r0 + half], zc], axis=1).astype(BF16)
            wq = mla_w_uq[j]
            zq = jnp.zeros((ql, nh, LANES - rope), F32)
            wq_rope = jnp.concatenate([wq[..., nope:], zq], axis=-1)
            wq_rot = jnp.concatenate([wq[..., nope + half:], wq[..., nope:nope + half], zq], axis=-1)
            wq = jnp.concatenate([wq[..., :nope].reshape(ql, -1), wq_rope.reshape(ql, -1),
                                  wq_rot.reshape(ql, -1)], axis=1).astype(BF16)
            wkv = jnp.concatenate([mla_w_uk[j].reshape(kvl, -1), mla_w_uv[j].reshape(kvl, -1)],
                                  axis=1).astype(BF16)
            w_ukt = mla_w_uk[j].transpose(1, 2, 0).astype(BF16)
            w_uvh = mla_w_uv[j].transpose(1, 0, 2).astype(BF16)
            w_o = mla_w_o[j].astype(BF16)
        for grp in range(2):
            x = xs[grp]
            b, l = bsz[grp], seq[grp]
            sh1, sc1, g1, sh2, sc2 = (mod_rows(i, grp, p) for p in range(5))
            lng, lnb = ln_g[i, 0].reshape(1, d), ln_b[i, 0].reshape(1, d)
            if i % 2 == 0:
                nbg = b // SUBLANES
                if grp == 0:
                    h0r = jnp.zeros((nbg, SUBLANES, n_groups * n_state), F32)
                    h0i = h0r
                else:
                    h0r = state_ssm_re[j].reshape(nbg, SUBLANES, n_groups * n_state)
                    h0i = state_ssm_im[j].reshape(nbg, SUBLANES, n_groups * n_state)
                z, hr, hi = _s5_scan(x, sc1, sh1, bd, cd, ab, ssm_d[j].reshape(1, d), h0r, h0i)
                ssm_out[grp][0].append(hr.reshape(b, n_groups, n_state))
                ssm_out[grp][1].append(hi.reshape(b, n_groups, n_state))

                def a_spec(tm):
                    return pl.BlockSpec((None, tm, d), lambda bi, t: (bi, t, 0))

                x1, tokens = _mix_ln(z, a_spec, w_glu, x, g1, lng, lnb, sc2, sh2, tokens, row0[grp],
                                     glu=True, alpha=alpha)
            else:
                cos_t, sin_t = tabs[grp]
                cq, ckv, kr, krp = _mla_in(h_in[grp], w_in_ext, mla_q_norm[j].reshape(1, ql),
                                           mla_kv_norm[j].reshape(1, kvl), cos_t, sin_t, ql=ql, kvl=kvl, rope=rope)
                q = _mla_q(cq, wq, cos_t, sin_t, nh=nh, scale=scale)
                if grp == 0:
                    kn, v = _mla_kv(ckv, wkv, nh=nh)
                    o = _prompt_attn(q, kn, krp, v)
                else:
                    q_lat = _absorb_q(q, w_ukt, nope=nope)
                    o_lat = _cached_attn(q_lat, q, cache_ckv[j], cache_krope[j], ckv, kr, nope=nope, rope=rope)
                    o = _absorb_v(o_lat, w_uvh)
                ckv_out[grp].append(ckv)
                kr_out[grp].append(kr)

                def a_spec(tm):
                    return pl.BlockSpec((None, tm, nh * vd), lambda bi, t: (bi, t, 0))

                x1, tokens = _mix_ln(o, a_spec, w_o, x, g1, lng, lnb, sc2, sh2, tokens, row0[grp],
                                     glu=False, alpha=alpha)
            x1s.append(x1)
        y_all, wt_all = _moe(tokens, w_router_b, router_bias, moe_w_gate, moe_w_up, moe_w_down, i)
        for grp in range(2):
            g2 = mod_rows(i, grp, 5)
            next_mod = (mod_rows(i + 1, grp, 1), mod_rows(i + 1, grp, 0)) if i % 2 == 0 else None
            xs[grp], h_in[grp] = _ffn_ln(y_all, wt_all, row0[grp], x1s[grp], g2, ln_g[i, 1].reshape(1, d),
                                         ln_b[i, 1].reshape(1, d), next_mod, alpha=alpha)

    return (xs[0], xs[1],
            jnp.stack(ssm_out[0][0]), jnp.stack(ssm_out[0][1]), jnp.stack(ckv_out[0]), jnp.stack(kr_out[0]),
            jnp.stack(ssm_out[1][0]), jnp.stack(ssm_out[1][1]), jnp.stack(ckv_out[1]), jnp.stack(kr_out[1]))
```

```python
import functools
import math

import jax
import jax.numpy as jnp
from jax import lax
from jax.experimental import pallas as pl
from jax.experimental.pallas import tpu as pltpu

F32 = jnp.float32
BF16 = jnp.bfloat16

CHUNK = 64
SSM_GROUP_WIDTH = 16
N_EXPERT_GROUPS = 8
TOP_K = 2
ROPE_THETA = 10000.0
LN_EPS = 1e-5
RMS_EPS = 1e-6
NEG_INF = -1e30

V7X_VMEM_BYTES = 64 * 1024 * 1024
SUBLANES = 8
LANES = 128
MXU_DIM = 256

VMEM_LIMIT = V7X_VMEM_BYTES * 7 // 8
S5_CHANNELS = MXU_DIM
S5_TIME = 256
ROW_TILE = 256
ATTN_BLOCK = 512
ATTN_HEADS = 4
MOE_ROWS = 256
CACHE_KEYS = 1024


def _cparams(sem):
    return pltpu.CompilerParams(dimension_semantics=sem, vmem_limit_bytes=VMEM_LIMIT)


def _bdot(a, b):
    return jnp.dot(a, b, preferred_element_type=F32)


def _ada_kernel(c_ref, w_ref, b_ref, o_ref):
    c = c_ref[...]
    cs = (c * jax.nn.sigmoid(c)).astype(BF16)
    o_ref[...] = _bdot(cs, w_ref[...].astype(BF16)) + b_ref[...]


def _ada_mod(c_all, w_ada, b_ada):
    depth, d, n = w_ada.shape
    bc = c_all.shape[0]
    tn = 1024
    return pl.pallas_call(
        _ada_kernel,
        grid=(depth, n // tn),
        in_specs=[pl.BlockSpec((bc, d), lambda i, j: (0, 0)),
                  pl.BlockSpec((None, d, tn), lambda i, j: (i, 0, j)),
                  pl.BlockSpec((None, 1, tn), lambda i, j: (i, 0, j))],
        out_specs=pl.BlockSpec((None, bc, tn), lambda i, j: (i, 0, j)),
        out_shape=jax.ShapeDtypeStruct((depth, bc, n), F32),
        compiler_params=_cparams(("parallel", "parallel")),
        name="ada_mod",
    )(c_all, w_ada, b_ada.reshape(depth, 1, n))


def _s5_kernel(x_ref, sc_ref, sh_ref, bd_ref, cd_ref, ab_ref, d_ref, h0r_ref, h0i_ref,
               z_ref, hr_ref, hi_ref, u_ref, xs_ref, hs_ref, h_ref, *, tt, sb):
    t = pl.program_id(2)

    @pl.when(t == 0)
    def _():
        h_ref[0] = h0r_ref[...]
        h_ref[1] = h0i_ref[...]

    lane_tiles = range(u_ref.shape[0])
    for b in range(SUBLANES):
        ub = x_ref[b] * (1.0 + sc_ref[b]) + sh_ref[b]
        for c in lane_tiles:
            u_ref[c, pl.ds(b, tt, stride=SUBLANES), :] = ub[:, c * LANES:(c + 1) * LANES]
    u = jnp.concatenate([u_ref[c] for c in lane_tiles], axis=1)
    xs_ref[...] = _bdot(u.astype(BF16), bd_ref[...])
    a_re = jnp.broadcast_to(ab_ref[0:1, :], (SUBLANES, sb))
    a_im = jnp.broadcast_to(ab_ref[1:2, :], (SUBLANES, sb))

    pair = 2 * SUBLANES

    def step(i, carry):
        h_re, h_im = carry
        r0 = pl.multiple_of(i * pair, pair)
        res, ims = [], []
        for s in range(2):
            x_re = xs_ref[pl.ds(r0 + s * SUBLANES, SUBLANES), 0:sb]
            x_im = xs_ref[pl.ds(r0 + s * SUBLANES, SUBLANES), sb:2 * sb]
            h_re, h_im = a_re * h_re - a_im * h_im + x_re, a_re * h_im + a_im * h_re + x_im
            res.append(h_re)
            ims.append(h_im)
        hs_ref[pl.ds(r0, pair), 0:sb] = jnp.concatenate(res, axis=0).astype(BF16)
        hs_ref[pl.ds(r0, pair), sb:2 * sb] = jnp.concatenate(ims, axis=0).astype(BF16)
        return h_re, h_im

    h_re, h_im = lax.fori_loop(0, tt // 2, step, (h_ref[0], h_ref[1]), unroll=2)
    h_ref[0] = h_re
    h_ref[1] = h_im
    y = _bdot(hs_ref[...], cd_ref[...]) + d_ref[...] * u
    zf = jax.nn.gelu(y)
    for c in lane_tiles:
        u_ref[c] = zf[:, c * LANES:(c + 1) * LANES]
    for b in range(SUBLANES):
        for c in lane_tiles:
            z_ref[b, :, c * LANES:(c + 1) * LANES] = (
                u_ref[c, pl.ds(b, tt, stride=SUBLANES), :].astype(z_ref.dtype))

    @pl.when(t == pl.num_programs(2) - 1)
    def _():
        hr_ref[...] = h_re
        hi_ref[...] = h_im


def _s5_scan(x, sc, sh, bd, cd, ab, d_skip, h0_re, h0_im):
    b, l, d = x.shape
    nbg = b // SUBLANES
    nj, cb, sb2 = bd.shape
    sb = sb2 // 2
    tt = min(l, S5_TIME)
    kern = functools.partial(_s5_kernel, tt=tt, sb=sb)
    tile = pl.BlockSpec((SUBLANES, tt, cb), lambda g, j, t: (g, t, j))
    row = pl.BlockSpec((SUBLANES, 1, cb), lambda g, j, t: (g, 0, j))
    state_spec = pl.BlockSpec((None, SUBLANES, sb), lambda g, j, t: (g, 0, j))
    state_shape = jax.ShapeDtypeStruct((nbg, SUBLANES, nj * sb), F32)
    return pl.pallas_call(
        kern,
        grid=(nbg, nj, l // tt),
        in_specs=[tile, row, row,
                  pl.BlockSpec((None, cb, sb2), lambda g, j, t: (j, 0, 0)),
                  pl.BlockSpec((None, sb2, cb), lambda g, j, t: (j, 0, 0)),
                  pl.BlockSpec((None, 2, sb), lambda g, j, t: (j, 0, 0)),
                  pl.BlockSpec((1, cb), lambda g, j, t: (0, j)),
                  state_spec, state_spec],
        out_specs=[tile, state_spec, state_spec],
        out_shape=[jax.ShapeDtypeStruct((b, l, d), BF16), state_shape, state_shape],
        scratch_shapes=[pltpu.VMEM((cb // LANES, tt * SUBLANES, LANES), F32), pltpu.VMEM((tt * SUBLANES, sb2), F32),
                        pltpu.VMEM((tt * SUBLANES, sb2), BF16), pltpu.VMEM((2, SUBLANES, sb), F32)],
        compiler_params=_cparams(("parallel", "parallel", "arbitrary")),
        name="s5_scan",
    )(x, sc, sh, bd, cd, ab, d_skip, h0_re, h0_im)


def _s5_tables(a_re, a_im, log_dt, b_re, b_im, c_re, c_im):
    g, p = a_re.shape
    w = b_re.shape[-1]
    gl = S5_CHANNELS // w
    nj = g // gl
    dt = jnp.exp(log_dt)[:, None]
    mag = jnp.exp(a_re * dt)
    abar_re, abar_im = mag * jnp.cos(a_im * dt), mag * jnp.sin(a_im * dt)
    den = a_re * a_re + a_im * a_im
    n_re, n_im = abar_re - 1.0, abar_im
    f_re = (n_re * a_re + n_im * a_im) / den
    f_im = (n_im * a_re - n_re * a_im) / den
    bb_re = f_re[..., None] * b_re - f_im[..., None] * b_im
    bb_im = f_re[..., None] * b_im + f_im[..., None] * b_re
    eye = jnp.eye(gl, dtype=F32)
    bb = jnp.stack([bb_re, bb_im]).reshape(2, nj, gl, p, w)
    bd = jnp.einsum('sjgpw,gh->jgwshp', bb, eye).reshape(nj, gl * w, 2 * gl * p)
    cc = jnp.stack([c_re, -c_im]).reshape(2, nj, gl, w, p)
    cd = jnp.einsum('sjgwp,gh->jsgphw', cc, eye).reshape(nj, 2 * gl * p, gl * w)
    ab = jnp.stack([abar_re.reshape(nj, gl * p), abar_im.reshape(nj, gl * p)], axis=1)
    return bd.astype(BF16), cd.astype(BF16), ab


def _layer_norm(y, g, b):
    mu = jnp.mean(y, axis=-1, keepdims=True)
    yc = y - mu
    var = jnp.mean(yc * yc, axis=-1, keepdims=True)
    return yc * lax.rsqrt(var + LN_EPS) * g + b


def _mix_ln_kernel(a_ref, w_ref, x_ref, gate_ref, lng_ref, lnb_ref, sc_ref, sh_ref, tokens_hbm,
                   x1_ref, h_ref, *, glu, alpha):
    del tokens_hbm
    acc = _bdot(a_ref[...], w_ref[...])
    if glu:
        d = acc.shape[-1] // 2
        m = acc[:, :d] * jax.nn.sigmoid(acc[:, d:])
    else:
        m = acc
    x1 = _layer_norm(alpha * x_ref[...] + gate_ref[...] * m, lng_ref[...], lnb_ref[...])
    x1_ref[...] = x1
    h_ref[...] = x1 * (1.0 + sc_ref[...]) + sh_ref[...]


def _mix_ln(a, a_spec, w, x, gate, ln_g, ln_b, sc, sh, tokens, row0, *, glu, alpha):
    b, l, d = x.shape
    tm = min(l, ROW_TILE)
    nt = l // tm
    blk0 = row0 // tm
    k, n = w.shape
    row = pl.BlockSpec((None, 1, d), lambda i, t: (i, 0, 0))
    vec = pl.BlockSpec((1, d), lambda i, t: (0, 0))
    tile = pl.BlockSpec((None, tm, d), lambda i, t: (i, t, 0))
    return pl.pallas_call(
        functools.partial(_mix_ln_kernel, glu=glu, alpha=alpha),
        grid=(b, nt),
        in_specs=[a_spec(tm), pl.BlockSpec((k, n), lambda i, t: (0, 0), pipeline_mode=pl.Buffered(1)),
                  tile, row, vec, vec, row, row, pl.BlockSpec(memory_space=pl.ANY)],
        out_specs=[tile, pl.BlockSpec((tm, d), lambda i, t: (blk0 + i * nt + t, 0))],
        out_shape=[jax.ShapeDtypeStruct((b, l, d), F32), jax.ShapeDtypeStruct(tokens.shape, F32)],
        input_output_aliases={8: 1},
        compiler_params=_cparams(("parallel", "parallel")),
        name="mix_ln_glu" if glu else "mix_ln_proj",
    )(a, w, x, gate, ln_g, ln_b, sc, sh, tokens)


def _ffn_ln_kernel(*refs, alpha):
    y_refs, (wt_ref, x_ref, gate_ref, lng_ref, lnb_ref), rest = refs[:TOP_K], refs[TOP_K:TOP_K + 5], refs[TOP_K + 5:]
    f = wt_ref[:, 0:1] * y_refs[0][...]
    for k in range(1, TOP_K):
        f = f + wt_ref[:, k:k + 1] * y_refs[k][...]
    x2 = _layer_norm(alpha * x_ref[...] + gate_ref[...] * f, lng_ref[...], lnb_ref[...])
    if len(rest) == 1:
        rest[0][...] = x2
    else:
        sc_ref, sh_ref, x2_ref, h_ref = rest
        x2_ref[...] = x2
        h_ref[...] = (x2 * (1.0 + sc_ref[...]) + sh_ref[...]).astype(h_ref.dtype)


def _ffn_ln(y_all, wt_all, row0, x, gate, ln_g, ln_b, next_mod, *, alpha):
    b, l, d = x.shape
    tm = min(l, ROW_TILE)
    nt = l // tm
    blk0 = row0 // tm
    kblk = wt_all.shape[0] // tm
    row = pl.BlockSpec((None, 1, d), lambda i, t: (i, 0, 0))
    vec = pl.BlockSpec((1, d), lambda i, t: (0, 0))
    tile = pl.BlockSpec((None, tm, d), lambda i, t: (i, t, 0))
    in_specs = [pl.BlockSpec((tm, d), functools.partial(lambda k, i, t: (k * kblk + blk0 + i * nt + t, 0), k))
                for k in range(TOP_K)]
    in_specs += [pl.BlockSpec((tm, TOP_K), lambda i, t: (blk0 + i * nt + t, 0)), tile, row, vec, vec]
    x2_shape = jax.ShapeDtypeStruct((b, l, d), F32)
    if next_mod is None:
        args, out_specs, out_shape = (), tile, x2_shape
    else:
        args, in_specs = tuple(next_mod), in_specs + [row, row]
        out_specs, out_shape = [tile, tile], [x2_shape, jax.ShapeDtypeStruct((b, l, d), BF16)]
    out = pl.pallas_call(
        functools.partial(_ffn_ln_kernel, alpha=alpha),
        grid=(b, nt),
        in_specs=in_specs,
        out_specs=out_specs,
        out_shape=out_shape,
        compiler_params=_cparams(("parallel", "parallel")),
        name="ffn_ln",
    )(*([y_all] * TOP_K), wt_all, x, gate, ln_g, ln_b, *args)
    return (out, None) if next_mod is None else out


def _rms_norm(x, g):
    ms = jnp.mean(x * x, axis=-1, keepdims=True)
    return x * lax.rsqrt(ms + RMS_EPS) * g


def _mla_in_kernel(h_ref, w_ref, qn_ref, kvn_ref, cos_ref, sin_ref,
                   cq_ref, ckv_ref, kr_ref, krp_ref, *, ql, kvl, rope):
    acc = _bdot(h_ref[...], w_ref[...])
    cq_ref[...] = _rms_norm(acc[:, :ql], qn_ref[...]).astype(cq_ref.dtype)
    ckv_ref[...] = _rms_norm(acc[:, ql:ql + kvl], kvn_ref[...])
    o = ql + kvl
    kr = acc[:, o:o + LANES] * cos_ref[...] + acc[:, o + LANES:o + 2 * LANES] * sin_ref[...]
    kr_ref[...] = kr[:, :rope]
    krp_ref[...] = kr.astype(krp_ref.dtype)


def _mla_in(h, w_in_ext, q_norm, kv_norm, cos_t, sin_t, *, ql, kvl, rope):
    b, l, d = h.shape
    tm = min(l, ROW_TILE)
    n = w_in_ext.shape[1]
    kern = functools.partial(_mla_in_kernel, ql=ql, kvl=kvl, rope=rope)
    tab = pl.BlockSpec((tm, LANES), lambda i, t: (t, 0))

    def out(nf):
        return pl.BlockSpec((None, tm, nf), lambda i, t: (i, t, 0))

    return pl.pallas_call(
        kern,
        grid=(b, l // tm),
        in_specs=[pl.BlockSpec((None, tm, d), lambda i, t: (i, t, 0)),
                  pl.BlockSpec((d, n), lambda i, t: (0, 0)),
                  pl.BlockSpec((1, ql), lambda i, t: (0, 0)),
                  pl.BlockSpec((1, kvl), lambda i, t: (0, 0)), tab, tab],
        out_specs=[out(ql), out(kvl), out(rope), out(LANES)],
        out_shape=[jax.ShapeDtypeStruct((b, l, ql), BF16), jax.ShapeDtypeStruct((b, l, kvl), F32),
                   jax.ShapeDtypeStruct((b, l, rope), F32), jax.ShapeDtypeStruct((b, l, LANES), BF16)],
        compiler_params=_cparams(("parallel", "parallel")),
        name="mla_in",
    )(h, w_in_ext, q_norm, kv_norm, cos_t, sin_t)


def _mla_q_kernel(cq_ref, w_ref, cos_ref, sin_ref, q_ref, *, nh, scale):
    acc = _bdot(cq_ref[...], w_ref[...])
    cos = cos_ref[...] * scale
    sin = sin_ref[...] * scale
    for h in range(nh):
        q_ref[h, :, :LANES] = (acc[:, h * LANES:(h + 1) * LANES] * scale).astype(q_ref.dtype)
        r0, p0 = (nh + h) * LANES, (2 * nh + h) * LANES
        q_ref[h, :, LANES:] = (acc[:, r0:r0 + LANES] * cos + acc[:, p0:p0 + LANES] * sin).astype(q_ref.dtype)


def _mla_q(cq, wq, cos_t, sin_t, *, nh, scale):
    b, l, ql = cq.shape
    n = wq.shape[1]
    tm = min(l, ROW_TILE)
    tab = pl.BlockSpec((tm, LANES), lambda i, t: (t, 0))
    return pl.pallas_call(
        functools.partial(_mla_q_kernel, nh=nh, scale=scale),
        grid=(b, l // tm),
        in_specs=[pl.BlockSpec((None, tm, ql), lambda i, t: (i, t, 0)),
                  pl.BlockSpec((ql, n), lambda i, t: (0, 0)), tab, tab],
        out_specs=pl.BlockSpec((None, nh, tm, 2 * LANES), lambda i, t: (i, 0, t, 0)),
        out_shape=jax.ShapeDtypeStruct((b, nh, l, 2 * LANES), BF16),
        compiler_params=_cparams(("parallel", "parallel")),
        name="mla_q",
    )(cq, wq, cos_t, sin_t)


def _mla_kv_kernel(ckv_ref, w_ref, k_ref, v_ref, *, nh):
    acc = _bdot(ckv_ref[...].astype(BF16), w_ref[...])
    for h in range(nh):
        k_ref[h] = acc[:, h * LANES:(h + 1) * LANES].astype(k_ref.dtype)
        v_ref[h] = acc[:, (nh + h) * LANES:(nh + h + 1) * LANES].astype(v_ref.dtype)


def _mla_kv(ckv, wkv, *, nh):
    b, l, kvl = ckv.shape
    n = wkv.shape[1]
    tm = min(l, ROW_TILE)
    head_tile = pl.BlockSpec((None, nh, tm, LANES), lambda i, t: (i, 0, t, 0))
    head_shape = jax.ShapeDtypeStruct((b, nh, l, LANES), BF16)
    return pl.pallas_call(
        functools.partial(_mla_kv_kernel, nh=nh),
        grid=(b, l // tm),
        in_specs=[pl.BlockSpec((None, tm, kvl), lambda i, t: (i, t, 0)),
                  pl.BlockSpec((kvl, n), lambda i, t: (0, 0))],
        out_specs=[head_tile, head_tile],
        out_shape=[head_shape, head_shape],
        compiler_params=_cparams(("parallel", "parallel")),
        name="mla_kv",
    )(ckv, wkv)


def _qk(q, k):
    return lax.dot_general(q, k, (((1,), (1,)), ((), ())), preferred_element_type=F32)


def _chunk_of(pos):
    if CHUNK & (CHUNK - 1) == 0:
        return lax.shift_right_logical(pos, CHUNK.bit_length() - 1)
    return pos // CHUNK


def _softmax_step(carry, s, v):
    m, l, acc = carry
    m_new = jnp.maximum(m, jnp.max(s, axis=-1, keepdims=True))
    p = jnp.exp2(s - m_new)
    alpha = jnp.exp2(m - m_new)
    l = alpha * l + jnp.sum(p, axis=-1, keepdims=True)
    acc = alpha * acc + _bdot(p.astype(BF16), v)
    return m_new, l, acc


def _softmax_step_ones(carry, s, v):
    m, acc = carry
    m_new = jnp.maximum(m, jnp.max(s, axis=-1, keepdims=True))
    p = jnp.exp2(s - m_new)
    acc = jnp.exp2(m - m_new) * acc + _bdot(p.astype(BF16), v)
    return m_new, acc


def _prompt_attn_kernel(q_ref, kn_ref, kr_ref, v_ref, o_ref, *, blk, vd):
    i = pl.program_id(2)
    heads = range(q_ref.shape[0])
    qs = [q_ref[h] for h in heads]
    lane = lax.broadcasted_iota(jnp.int32, (blk, LANES), 1)
    ones_col = jnp.where(lane == 0, 1.0, 0.0).astype(v_ref.dtype)

    def keys(h, r):
        return jnp.concatenate([kn_ref[h, r, :], kr_ref[r, :]], axis=1)

    def values(h, r):
        return jnp.concatenate([v_ref[h, r, :], ones_col], axis=1)

    def body(j, carry):
        r = pl.ds(pl.multiple_of(j * blk, blk), blk)
        return tuple(_softmax_step_ones(carry[h], _qk(qs[h], keys(h, r)), values(h, r)) for h in heads)

    init = (jnp.full((blk, 1), NEG_INF, F32), jnp.zeros((blk, 2 * LANES), F32))
    carry = lax.fori_loop(0, i, body, tuple(init for _ in heads))
    r = pl.ds(pl.multiple_of(i * blk, blk), blk)
    q_chunk = _chunk_of(lax.broadcasted_iota(jnp.int32, (blk, blk), 0))
    k_chunk = _chunk_of(lax.broadcasted_iota(jnp.int32, (blk, blk), 1))
    visible = k_chunk <= q_chunk
    for h in heads:
        s = jnp.where(visible, _qk(qs[h], keys(h, r)), NEG_INF)
        _, acc = _softmax_step_ones(carry[h], s, values(h, r))
        o_ref[:, h * vd:(h + 1) * vd] = (acc[:, :vd] / acc[:, vd:vd + 1]).astype(o_ref.dtype)


def _prompt_attn(q, kn, kr, v):
    b, nh, l, dk = q.shape
    vd = v.shape[-1]
    blk = min(l, ATTN_BLOCK)
    hp = ATTN_HEADS
    per_head = pl.BlockSpec((None, hp, l, LANES), lambda i, h, t: (i, h, 0, 0))
    return pl.pallas_call(
        functools.partial(_prompt_attn_kernel, blk=blk, vd=vd),
        grid=(b, nh // hp, l // blk),
        in_specs=[pl.BlockSpec((None, hp, blk, dk), lambda i, h, t: (i, h, t, 0)), per_head,
                  pl.BlockSpec((None, l, LANES), lambda i, h, t: (i, 0, 0)), per_head],
        out_specs=pl.BlockSpec((None, blk, hp * vd), lambda i, h, t: (i, t, h)),
        out_shape=jax.ShapeDtypeStruct((b, l, nh * vd), BF16),
        compiler_params=_cparams(("parallel", "parallel", "arbitrary")),
        name="prompt_attn",
    )(q, kn, kr, v)


def _absorb_q_kernel(q_ref, w_ref, o_ref, *, nope):
    b, lq, dk = q_ref.shape
    q = q_ref[...].reshape(b * lq, dk)[:, :nope]
    o_ref[...] = _bdot(q, w_ref[...]).reshape(b, lq, -1).astype(o_ref.dtype)


def _absorb_q(q, w_ukt, *, nope):
    b, nh, lq, dk = q.shape
    kvl = w_ukt.shape[-1]
    return pl.pallas_call(
        functools.partial(_absorb_q_kernel, nope=nope),
        grid=(nh,),
        in_specs=[pl.BlockSpec((b, None, lq, dk), lambda h: (0, h, 0, 0)),
                  pl.BlockSpec((None, nope, kvl), lambda h: (h, 0, 0))],
        out_specs=pl.BlockSpec((b, None, lq, kvl), lambda h: (0, h, 0, 0)),
        out_shape=jax.ShapeDtypeStruct((b, nh, lq, kvl), BF16),
        compiler_params=_cparams(("parallel",)),
        name="absorb_q",
    )(q, w_ukt)


def _cached_attn_kernel(ql_ref, q_ref, cc_ref, ck_ref, nc_ref, nk_ref, o_ref, *, nope, rope, past, kb):
    nh, lq, kvl = ql_ref.shape
    rows = nh * lq
    ql = ql_ref[...].reshape(rows, kvl)
    qr = q_ref[...].reshape(rows, -1)[:, nope:nope + rope]
    q_pos = past + lax.broadcasted_iota(jnp.int32, (rows, 1), 0) % lq
    carry = (jnp.full((rows, 1), NEG_INF, F32), jnp.zeros((rows, 1), F32), jnp.zeros((rows, kvl), F32))

    def scores(c, kr, k0):
        s = _qk(ql, c) + _qk(qr, kr)
        k_pos = k0 + lax.broadcasted_iota(jnp.int32, s.shape, 1)
        return jnp.where(_chunk_of(k_pos) <= _chunk_of(q_pos), s, NEG_INF)

    for j in range(past // kb):
        c = cc_ref[j * kb:(j + 1) * kb, :].astype(BF16)
        kr = ck_ref[j * kb:(j + 1) * kb, :].astype(BF16)
        carry = _softmax_step(carry, scores(c, kr, j * kb), c)
    c = nc_ref[...].astype(BF16)
    kr = nk_ref[...].astype(BF16)
    _, l, acc = _softmax_step(carry, scores(c, kr, past), c)
    o_ref[...] = (acc / l).reshape(nh, lq, kvl).astype(o_ref.dtype)


def _cached_attn(q_lat, q, cache_ckv, cache_kr, new_ckv, new_kr, *, nope, rope):
    b, nh, lq, kvl = q_lat.shape
    past = cache_ckv.shape[1]
    kb = min(past, CACHE_KEYS)
    dk = q.shape[-1]
    kern = functools.partial(_cached_attn_kernel, nope=nope, rope=rope, past=past, kb=kb)
    return pl.pallas_call(
        kern,
        grid=(b,),
        in_specs=[pl.BlockSpec((None, nh, lq, kvl), lambda i: (i, 0, 0, 0)),
                  pl.BlockSpec((None, nh, lq, dk), lambda i: (i, 0, 0, 0)),
                  pl.BlockSpec((None, past, kvl), lambda i: (i, 0, 0)),
                  pl.BlockSpec((None, past, rope), lambda i: (i, 0, 0)),
                  pl.BlockSpec((None, lq, kvl), lambda i: (i, 0, 0)),
                  pl.BlockSpec((None, lq, rope), lambda i: (i, 0, 0))],
        out_specs=pl.BlockSpec((None, nh, lq, kvl), lambda i: (i, 0, 0, 0)),
        out_shape=jax.ShapeDtypeStruct((b, nh, lq, kvl), BF16),
        compiler_params=_cparams(("parallel",)),
        name="cached_attn",
    )(q_lat, q, cache_ckv, cache_kr, new_ckv, new_kr)


def _absorb_v_kernel(o_ref, w_ref, out_ref):
    b, lq, kvl = o_ref.shape
    out_ref[...] = _bdot(o_ref[...].reshape(b * lq, kvl), w_ref[...]).reshape(b, lq, -1).astype(out_ref.dtype)


def _absorb_v(o_lat, w_uv):
    b, nh, lq, kvl = o_lat.shape
    vd = w_uv.shape[-1]
    return pl.pallas_call(
        _absorb_v_kernel,
        grid=(nh,),
        in_specs=[pl.BlockSpec((b, None, lq, kvl), lambda h: (0, h, 0, 0)),
                  pl.BlockSpec((None, kvl, vd), lambda h: (h, 0, 0))],
        out_specs=pl.BlockSpec((b, lq, vd), lambda h: (0, 0, h)),
        out_shape=jax.ShapeDtypeStruct((b, lq, nh * vd), BF16),
        compiler_params=_cparams(("parallel",)),
        name="absorb_v",
    )(o_lat, w_uv)


def _router_kernel(x_ref, w_ref, o_ref):
    o_ref[...] = _bdot(x_ref[...].astype(BF16), w_ref[...])


def _router_logits(x, w):
    t, d = x.shape
    e = w.shape[1]
    tm = math.gcd(t, 1024)
    return pl.pallas_call(
        _router_kernel,
        grid=(t // tm,),
        in_specs=[pl.BlockSpec((tm, d), lambda i: (i, 0)), pl.BlockSpec((d, e), lambda i: (0, 0))],
        out_specs=pl.BlockSpec((tm, e), lambda i: (i, 0)),
        out_shape=jax.ShapeDtypeStruct((t, e), F32),
        compiler_params=_cparams(("parallel",)),
        name="router_logits",
    )(x, w)


def _expert_kernel(be_ref, nv_ref, nb_ref, slot_ref, x_hbm, wg_ref, wu_ref, wd_ref, y_hbm,
                   xbuf, obuf, wgb_ref, wub_ref, wdb_ref, gsem, ssem, *, bm):
    i = pl.program_id(0)
    n_used = nb_ref[0]
    cur = lax.rem(i, 2)
    n_tok = x_hbm.shape[0]

    def token_of(slot):
        if TOP_K & (TOP_K - 1) == 0:
            return lax.shift_right_logical(slot, TOP_K.bit_length() - 1)
        return lax.div(slot, TOP_K)

    def start_gather(blk, buf):
        base = blk * bm
        for r in range(bm):
            tok = token_of(slot_ref[base + r])
            pltpu.make_async_copy(x_hbm.at[pl.ds(tok, 1), :], xbuf.at[buf, pl.ds(r, 1), :], gsem.at[buf]).start()

    def scatter_row(base, buf, r):
        slot = slot_ref[base + r]
        tok = token_of(slot)
        dst = (slot - tok * TOP_K) * n_tok + tok
        pltpu.make_async_copy(obuf.at[buf, pl.ds(r, 1), :], y_hbm.at[pl.ds(dst, 1), :], ssem.at[buf]).start()

    def start_scatter(blk, buf):
        base = blk * bm
        nv = nv_ref[blk]

        @pl.when(nv == bm)
        def _():
            for r in range(bm):
                scatter_row(base, buf, r)

        @pl.when(nv < bm)
        def _():
            def group(g, carry):
                r0 = pl.multiple_of(g * SUBLANES, SUBLANES)
                for j in range(SUBLANES):
                    scatter_row(base, buf, r0 + j)
                return carry
            full = lax.shift_right_logical(nv, SUBLANES.bit_length() - 1)
            lax.fori_loop(0, full, group, 0)

            def single(r, carry):
                scatter_row(base, buf, r)
                return carry
            lax.fori_loop(full * SUBLANES, nv, single, 0)

    def wait_scatter(blk, buf):
        n = nv_ref[blk]
        piece = bm
        while piece >= 1:
            @pl.when((n & piece) != 0)
            def _(piece=piece):
                rows, times = (piece, 1) if piece >= SUBLANES else (1, piece)
                for _ in range(times):
                    pltpu.make_async_copy(obuf.at[buf, pl.ds(0, rows), :], y_hbm.at[pl.ds(0, rows), :],
                                          ssem.at[buf]).wait()
            piece //= 2

    @pl.when(i == 0)
    def _():
        start_gather(0, 0)

    def wait_gather(buf):
        pltpu.make_async_copy(x_hbm.at[pl.ds(0, bm), :], xbuf.at[buf], gsem.at[buf]).wait()

    @pl.when(i < n_used)
    def _():
        wait_gather(cur)

        @pl.when(jnp.logical_or(i == 0, be_ref[i] != be_ref[jnp.maximum(i - 1, 0)]))
        def _():
            wgb_ref[...] = wg_ref[...].astype(BF16)
            wub_ref[...] = wu_ref[...].astype(BF16)
            wdb_ref[...] = wd_ref[...].astype(BF16)

        @pl.when(i >= 2)
        def _():
            wait_scatter(i - 2, cur)

        start_gather(jnp.minimum(i + 1, n_used - 1), 1 - cur)
        x = xbuf[cur].astype(BF16)
        g = _bdot(x, wgb_ref[...])
        h = (g * jax.nn.sigmoid(g)) * _bdot(x, wub_ref[...])
        obuf[cur] = _bdot(h.astype(BF16), wdb_ref[...])
        start_scatter(i, cur)

    @pl.when(i == pl.num_programs(0) - 1)
    def _():
        wait_gather(lax.rem(n_used, 2))

        @pl.when(n_used >= 2)
        def _():
            wait_scatter(n_used - 2, lax.rem(n_used, 2))
        wait_scatter(n_used - 1, lax.rem(n_used - 1, 2))


def _expert_mlp(block_e, n_valid, n_used, row_slot, x, w_gate, w_up, w_down, layer):
    t, d = x.shape
    ff = w_gate.shape[-1]
    bm = MOE_ROWS
    n_blocks = block_e.shape[0]
    grid_spec = pltpu.PrefetchScalarGridSpec(
        num_scalar_prefetch=4,
        grid=(n_blocks,),
        in_specs=[pl.BlockSpec(memory_space=pl.ANY),
                  pl.BlockSpec((None, None, d, ff), lambda i, be, nv, nb, sl: (layer, be[i], 0, 0)),
                  pl.BlockSpec((None, None, d, ff), lambda i, be, nv, nb, sl: (layer, be[i], 0, 0)),
                  pl.BlockSpec((None, None, ff, d), lambda i, be, nv, nb, sl: (layer, be[i], 0, 0))],
        out_specs=pl.BlockSpec(memory_space=pl.ANY),
        scratch_shapes=[pltpu.VMEM((2, bm, d), F32), pltpu.VMEM((2, bm, d), F32),
                        pltpu.VMEM((d, ff), BF16), pltpu.VMEM((d, ff), BF16), pltpu.VMEM((ff, d), BF16),
                        pltpu.SemaphoreType.DMA((2,)), pltpu.SemaphoreType.DMA((2,))],
    )
    return pl.pallas_call(
        functools.partial(_expert_kernel, bm=bm),
        grid_spec=grid_spec,
        out_shape=jax.ShapeDtypeStruct((TOP_K * t, d), F32),
        compiler_params=_cparams(("arbitrary",)),
        name="expert_mlp",
    )(block_e, n_valid, n_used, row_slot, x, w_gate, w_up, w_down)


def _first_argmax(x):
    return jnp.argmax(x, axis=-1).astype(jnp.int32)


def _moe(h, w_router, router_bias, w_gate, w_up, w_down, layer):
    t, d = h.shape
    ne = w_router.shape[1]
    ng = N_EXPERT_GROUPS
    eg = ne // ng
    scores = jax.nn.sigmoid(_router_logits(h, w_router))
    sel = scores + router_bias
    grp = sel.reshape(t, ng, eg)
    lane = jnp.arange(eg, dtype=jnp.int32)
    i1 = _first_argmax(grp)
    m1 = jnp.max(grp, axis=-1)
    m2 = jnp.max(jnp.where(lane == i1[..., None], -jnp.inf, grp), axis=-1)
    g_idx = _first_argmax(m1 + m2)
    sel_g = jnp.take_along_axis(grp, g_idx[:, None, None], axis=1)[:, 0]
    sc_g = jnp.take_along_axis(scores.reshape(t, ng, eg), g_idx[:, None, None], axis=1)[:, 0]
    e1 = _first_argmax(sel_g)
    e2 = _first_argmax(jnp.where(lane == e1[:, None], -jnp.inf, sel_g))
    w1 = jnp.take_along_axis(sc_g, e1[:, None], axis=1)[:, 0]
    w2 = jnp.take_along_axis(sc_g, e2[:, None], axis=1)[:, 0]
    wsum = w1 + w2
    e_idx = jnp.stack([g_idx * eg + e1, g_idx * eg + e2], axis=1)
    w = jnp.stack([w1 / wsum, w2 / wsum], axis=1)

    bm = MOE_ROWS
    n_assign = t * TOP_K
    flat_e = e_idx.reshape(-1)
    order = jnp.argsort(flat_e).astype(jnp.int32)
    counts = jnp.sum((flat_e[:, None] == jnp.arange(ne, dtype=jnp.int32)[None, :]).astype(jnp.int32), axis=0)
    starts = jnp.cumsum(counts) - counts
    pcounts = (counts + bm - 1) // bm * bm
    pends = jnp.cumsum(pcounts)
    pstarts = pends - pcounts
    n_blocks = -(-n_assign // bm) + ne
    blk_start = jnp.arange(n_blocks, dtype=jnp.int32) * bm
    block_e = jnp.minimum(jnp.sum((pends[None, :] <= blk_start[:, None]).astype(jnp.int32), axis=1), ne - 1)
    n_used = (pends[ne - 1:] // bm).astype(jnp.int32)
    blk_off = blk_start - pstarts[block_e]
    n_valid = jnp.clip(counts[block_e] - blk_off, 0, bm).astype(jnp.int32)
    off = blk_off[:, None] + jnp.arange(bm, dtype=jnp.int32)[None, :]
    src = jnp.clip(starts[block_e][:, None] + off, 0, n_assign - 1)
    row_slot = jnp.where(off < counts[block_e][:, None], order[src.reshape(-1)].reshape(n_blocks, bm), 0)
    y = _expert_mlp(block_e, n_valid, n_used, row_slot.reshape(-1), h, w_gate, w_up, w_down, layer)
    return y, w


def _rope_tables(pos, rope):
    inv = 1.0 / (ROPE_THETA ** (jnp.arange(0, rope, 2, dtype=F32) / rope))
    ang = pos.astype(F32)[:, None] * inv[None, :]
    cos, sin = jnp.cos(ang), jnp.sin(ang)
    pad = jnp.zeros((pos.shape[0], LANES - rope), F32)
    return jnp.concatenate([cos, cos, pad], axis=1), jnp.concatenate([-sin, sin, pad], axis=1)


def kernel(x_prompt, x_sample, c_prompt, c_sample, state_ssm_re, state_ssm_im, cache_ckv, cache_krope, w_ada, b_ada, ln_g, ln_b, ssm_a_re, ssm_a_im, ssm_log_dt, ssm_b_re, ssm_b_im, ssm_c_re, ssm_c_im, ssm_d, ssm_w_glu, mla_w_in, mla_q_norm, mla_kv_norm, mla_w_uq, mla_w_uk, mla_w_uv, mla_w_o, w_router, router_bias, moe_w_gate, moe_w_up, moe_w_down):
    depth, d = w_ada.shape[0], w_ada.shape[1]
    alpha = (2 * depth) ** 0.25
    nh, nope, vd = mla_w_uk.shape[2], mla_w_uk.shape[3], mla_w_uv.shape[3]
    ql, kvl = mla_q_norm.shape[1], mla_kv_norm.shape[1]
    rope = mla_w_uq.shape[3] - nope
    assert nope == LANES and vd == LANES and rope <= LANES
    scale = (nope + rope) ** -0.5 * math.log2(math.e)
    n_groups, n_state = ssm_a_re.shape[1], ssm_a_re.shape[2]
    past = cache_ckv.shape[2]

    xs = [x_prompt, x_sample]
    bsz = [x.shape[0] for x in xs]
    seq = [x.shape[1] for x in xs]
    row0 = [0, bsz[0] * seq[0]]
    n_tokens = row0[1] + bsz[1] * seq[1]
    pos = [jnp.arange(seq[0]), past + jnp.arange(seq[1])]
    tabs = [_rope_tables(p, rope) for p in pos]

    mod = _ada_mod(jnp.concatenate([c_prompt, c_sample], axis=0), w_ada, b_ada)

    def mod_rows(i, grp, piece):
        b0 = 0 if grp == 0 else bsz[0]
        return mod[i, b0:b0 + bsz[grp], piece * d:(piece + 1) * d].reshape(bsz[grp], 1, d)

    assert depth % 2 == 0, "layers alternate S5 / MLA mixers, starting with S5"
    w_router_b = w_router.astype(BF16)
    half = rope // 2
    ssm_out = [[[], []], [[], []]]
    ckv_out, kr_out = [[], []], [[], []]

    h_in = [None, None]
    for i in range(depth):
        j = i // 2
        x1s, tokens = [], jnp.zeros((n_tokens, d), F32)
        if i % 2 == 0:
            bd, cd, ab = _s5_tables(ssm_a_re[j], ssm_a_im[j], ssm_log_dt[j], ssm_b_re[j], ssm_b_im[j],
                                    ssm_c_re[j], ssm_c_im[j])
            w_glu = ssm_w_glu[j].astype(BF16)
        else:
            w_in = mla_w_in[j]
            r0 = ql + kvl
            zc = jnp.zeros((d, LANES - rope), F32)
            w_in_ext = jnp.concatenate([w_in, zc, w_in[:, r0 + half:], w_in[:, r0:r0 + half], zc], axis=1).astype(BF16)
            wq = mla_w_uq[j]
            zq = jnp.zeros((ql, nh, LANES - rope), F32)
            wq_rope = jnp.concatenate([wq[..., nope:], zq], axis=-1)
            wq_rot = jnp.concatenate([wq[..., nope + half:], wq[..., nope:nope + half], zq], axis=-1)
            wq = jnp.concatenate([wq[..., :nope].reshape(ql, -1), wq_rope.reshape(ql, -1),
                                  wq_rot.reshape(ql, -1)], axis=1).astype(BF16)
            wkv = jnp.concatenate([mla_w_uk[j].reshape(kvl, -1), mla_w_uv[j].reshape(kvl, -1)],
                                  axis=1).astype(BF16)
            w_ukt = mla_w_uk[j].transpose(1, 2, 0).astype(BF16)
            w_uvh = mla_w_uv[j].transpose(1, 0, 2).astype(BF16)
            w_o = mla_w_o[j].astype(BF16)
        for grp in range(2):
            x = xs[grp]
            b, l = bsz[grp], seq[grp]
            sh1, sc1, g1, sh2, sc2 = (mod_rows(i, grp, p) for p in range(5))
            lng, lnb = ln_g[i, 0].reshape(1, d), ln_b[i, 0].reshape(1, d)
            if i % 2 == 0:
                nbg = b // SUBLANES
                if grp == 0:
                    h0r = jnp.zeros((nbg, SUBLANES, n_groups * n_state), F32)
                    h0i = h0r
                else:
                    h0r = state_ssm_re[j].reshape(nbg, SUBLANES, n_groups * n_state)
                    h0i = state_ssm_im[j].reshape(nbg, SUBLANES, n_groups * n_state)
                z, hr, hi = _s5_scan(x, sc1, sh1, bd, cd, ab, ssm_d[j].reshape(1, d), h0r, h0i)
                ssm_out[grp][0].append(hr.reshape(b, n_groups, n_state))
                ssm_out[grp][1].append(hi.reshape(b, n_groups, n_state))

                def a_spec(tm):
                    return pl.BlockSpec((None, tm, d), lambda bi, t: (bi, t, 0))

                x1, tokens = _mix_ln(z, a_spec, w_glu, x, g1, lng, lnb, sc2, sh2, tokens, row0[grp],
                                     glu=True, alpha=alpha)
            else:
                cos_t, sin_t = tabs[grp]
                cq, ckv, kr, krp = _mla_in(h_in[grp], w_in_ext, mla_q_norm[j].reshape(1, ql),
                                           mla_kv_norm[j].reshape(1, kvl), cos_t, sin_t, ql=ql, kvl=kvl, rope=rope)
                q = _mla_q(cq, wq, cos_t, sin_t, nh=nh, scale=scale)
                if grp == 0:
                    kn, v = _mla_kv(ckv, wkv, nh=nh)
                    o = _prompt_attn(q, kn, krp, v)
                else:
                    q_lat = _absorb_q(q, w_ukt, nope=nope)
                    o_lat = _cached_attn(q_lat, q, cache_ckv[j], cache_krope[j], ckv, kr, nope=nope, rope=rope)
                    o = _absorb_v(o_lat, w_uvh)
                ckv_out[grp].append(ckv)
                kr_out[grp].append(kr)

                def a_spec(tm):
                    return pl.BlockSpec((None, tm, nh * vd), lambda bi, t: (bi, t, 0))

                x1, tokens = _mix_ln(o, a_spec, w_o, x, g1, lng, lnb, sc2, sh2, tokens, row0[grp],
                                     glu=False, alpha=alpha)
            x1s.append(x1)
        y_all, wt_all = _moe(tokens, w_router_b, router_bias, moe_w_gate, moe_w_up, moe_w_down, i)
        for grp in range(2):
            g2 = mod_rows(i, grp, 5)
            next_mod = (mod_rows(i + 1, grp, 1), mod_rows(i + 1, grp, 0)) if i % 2 == 0 else None
            xs[grp], h_in[grp] = _ffn_ln(y_all, wt_all, row0[grp], x1s[grp], g2, ln_g[i, 1].reshape(1, d),
                                         ln_b[i, 1].reshape(1, d), next_mod, alpha=alpha)

    return (xs[0], xs[1],
            jnp.stack(ssm_out[0][0]), jnp.stack(ssm_out[0][1]), jnp.stack(ckv_out[0]), jnp.stack(kr_out[0]),
            jnp.stack(ssm_out[1][0]), jnp.stack(ssm_out[1][1]), jnp.stack(ckv_out[1]), jnp.stack(kr_out[1]))
```
